```python
import jax, jax.numpy as jnp
from jax import lax
import numpy as np

D_MODEL = 1024
BATCH = 2
SEQ = 8192
DEPTH = 1
DEC_BATCH = 2
DEC_SEQ = 16384
PAST_LEN = 128

HEAD_DIM = 64
DILATED_PAIRS = ((128, 1), (512, 4), (2048, 16))
A_HEADS_PER_GROUP = 4
A_GROUPS = len(DILATED_PAIRS)
A_HEADS = A_HEADS_PER_GROUP * A_GROUPS
A_OUT = A_HEADS_PER_GROUP * HEAD_DIM
B_HEADS = 8
A_WIDTH = A_HEADS * HEAD_DIM
B_WIDTH = B_HEADS * HEAD_DIM
GATE_WIDTH = 2 * D_MODEL
IN_WIDTH = 3 * A_WIDTH + 3 * B_WIDTH + GATE_WIDTH
GRID_W = 64
NA_ROWS = 8
NA_COLS = 16
ROPE_THETA = 500000.0
ROT_DIM = HEAD_DIM // 4
D_FF = 2816
RMS_EPS = 1e-6
NEG = -1e30

kernel_name = "hybrid_dilated_neighbourhood_encoder"


def rms_norm(x, g):
    xf = x.astype(jnp.float32)
    y = xf * lax.rsqrt(jnp.mean(xf * xf, axis=-1, keepdims=True) + RMS_EPS)
    return (y * g.astype(jnp.float32)).astype(x.dtype)


def swiglu(h, w_gate, w_up, w_down):
    return (jax.nn.silu(h @ w_gate) * (h @ w_up)) @ w_down


def partial_rotary(x):
    S = x.shape[1]
    half = ROT_DIM // 2
    inv_freq = ROPE_THETA ** (-jnp.arange(0, ROT_DIM, 2, dtype=jnp.float32) / ROT_DIM)
    ang = jnp.arange(S, dtype=jnp.float32)[:, None] * inv_freq[None, :]
    cos = jnp.cos(ang)[None, :, None, :]
    sin = jnp.sin(ang)[None, :, None, :]
    xr = x[..., :ROT_DIM].astype(jnp.float32)
    x1, x2 = xr[..., :half], xr[..., half:]
    rot = jnp.concatenate([x1 * cos - x2 * sin, x2 * cos + x1 * sin], axis=-1).astype(x.dtype)
    return jnp.concatenate([rot, x[..., ROT_DIM:]], axis=-1)


def dilated_window_group(q, k, v, window, dilation):
    B, S, H, E = q.shape
    R = window // (2 * dilation)
    L = S // dilation
    nb = -(-L // R)
    Lp = nb * R

    def to_sub(x):
        return x.reshape(B, L, dilation, H, E).transpose(0, 2, 1, 3, 4)

    qb = jnp.pad(to_sub(q), ((0, 0), (0, 0), (0, Lp - L), (0, 0), (0, 0)))
    qb = qb.reshape(B, dilation, nb, R, H, E)

    def windows(x):
        xp = jnp.pad(to_sub(x), ((0, 0), (0, 0), (R, Lp - L + R), (0, 0), (0, 0)))
        xb = xp.reshape(B, dilation, nb + 2, R, H, E)
        return jnp.concatenate([xb[:, :, :-2], xb[:, :, 1:-1], xb[:, :, 2:]], axis=3)

    kw = windows(k)
    vw = windows(v)
    s = jnp.einsum('bdnqhe,bdnkhe->bdnhqk', qb, kw).astype(jnp.float32) * (E ** -0.5)
    qpos = (jnp.arange(nb) * R)[:, None] + jnp.arange(R)[None, :]
    kpos = (jnp.arange(nb) * R - R)[:, None] + jnp.arange(3 * R)[None, :]
    valid = (jnp.abs(kpos[:, None, :] - qpos[:, :, None]) <= R) \
        & (kpos >= 0)[:, None, :] & (kpos < L)[:, None, :]
    s = jnp.where(valid[None, None, :, None], s, NEG)
    lse = jax.nn.logsumexp(s, axis=-1)
    p = jnp.exp(s - lse[..., None])
    o = jnp.einsum('bdnhqk,bdnkhe->bdnqhe', p.astype(v.dtype), vw)
    o = o.reshape(B, dilation, Lp, H, E)[:, :, :L]
    o = o.transpose(0, 2, 1, 3, 4).reshape(B, S, H, E)
    lse = lse.transpose(0, 1, 2, 4, 3).reshape(B, dilation, Lp, H)[:, :, :L]
    lse = lse.transpose(0, 2, 1, 3).reshape(B, S, H)
    return o, lse


def neighbourhood_attention(q, k, v, rpb):
    B, S, H, E = q.shape
    rows = S // GRID_W
    kh = min(NA_ROWS, rows)
    qg = q.reshape(B, rows, GRID_W, H, E)
    kg = k.reshape(B, rows, GRID_W, H, E)
    vg = v.reshape(B, rows, GRID_W, H, E)
    ri = jnp.arange(rows)
    rs = jnp.clip(ri - kh // 2, 0, rows - kh)
    row_idx = rs[:, None] + jnp.arange(kh)[None, :]
    k_rows = kg[:, row_idx]
    v_rows = vg[:, row_idx]
    cj = jnp.arange(GRID_W)
    cs = jnp.clip(cj - NA_COLS // 2, 0, GRID_W - NA_COLS)
    col_valid = (cj[None, :] >= cs[:, None]) & (cj[None, :] < cs[:, None] + NA_COLS)
    dr_idx = row_idx - ri[:, None] + (NA_ROWS - 1)
    dc_idx = jnp.clip(cj[None, :] - cj[:, None] + (NA_COLS - 1), 0, 2 * NA_COLS - 2)
    bias = rpb[:, dr_idx[:, None, :, None], dc_idx[None, :, None, :]]
    bias = bias.transpose(1, 0, 2, 3, 4).astype(jnp.float32)
    s = jnp.einsum('brqhe,brakhe->brhqak', qg, k_rows).astype(jnp.float32) * (E ** -0.5)
    s = jnp.where(col_valid[:, None, :], s + bias[None], NEG)
    p = jax.nn.softmax(s.reshape(B, rows, H, GRID_W, kh * GRID_W), axis=-1)
    p = p.reshape(B, rows, H, GRID_W, kh, GRID_W).astype(v.dtype)
    o = jnp.einsum('brhqak,brakhe->brqhe', p, v_rows)
    return o.reshape(B, S, H, E)


def token_mixer(h, w_in, b_gate, rpb, w_branch_a, w_branch_b, w_out):
    B, S, _ = h.shape
    proj = h @ w_in
    o0 = 0
    qa = proj[..., o0:o0 + A_WIDTH].reshape(B, S, A_HEADS, HEAD_DIM); o0 += A_WIDTH
    ka = proj[..., o0:o0 + A_WIDTH].reshape(B, S, A_HEADS, HEAD_DIM); o0 += A_WIDTH
    va = proj[..., o0:o0 + A_WIDTH].reshape(B, S, A_HEADS, HEAD_DIM); o0 += A_WIDTH
    qb = proj[..., o0:o0 + B_WIDTH].reshape(B, S, B_HEADS, HEAD_DIM); o0 += B_WIDTH
    kb = proj[..., o0:o0 + B_WIDTH].reshape(B, S, B_HEADS, HEAD_DIM); o0 += B_WIDTH
    vb = proj[..., o0:o0 + B_WIDTH].reshape(B, S, B_HEADS, HEAD_DIM); o0 += B_WIDTH
    g = proj[..., o0:o0 + GATE_WIDTH] + b_gate

    qa = partial_rotary(qa)
    ka = partial_rotary(ka)
    outs, lses = [], []
    for gi, (window, dilation) in enumerate(DILATED_PAIRS):
        sl = slice(gi * A_HEADS_PER_GROUP, (gi + 1) * A_HEADS_PER_GROUP)
        o, lse = dilated_window_group(qa[:, :, sl], ka[:, :, sl], va[:, :, sl], window, dilation)
        outs.append(o)
        lses.append(lse)
    alpha = jax.nn.softmax(jnp.stack(lses, axis=0), axis=0)
    oa = jnp.sum(alpha[..., None] * jnp.stack(outs, axis=0).astype(jnp.float32), axis=0)
    ya = oa.astype(h.dtype).reshape(B, S, A_OUT) @ w_branch_a

    ob = neighbourhood_attention(qb, kb, vb, rpb)
    yb = ob.reshape(B, S, B_WIDTH) @ w_branch_b

    gates = jax.nn.sigmoid(g.astype(jnp.float32)).astype(h.dtype)
    ga, gb = gates[..., :D_MODEL], gates[..., D_MODEL:]
    return (ga * ya + gb * yb) @ w_out


def encoder_stack(x, ffn1_pre_g, ffn1_post_g, ffn1_w_gate, ffn1_w_up, ffn1_w_down,
                  mix_pre_g, mix_post_g, w_in, b_gate, rpb, w_branch_a, w_branch_b, w_out,
                  ffn2_pre_g, ffn2_post_g, ffn2_w_gate, ffn2_w_up, ffn2_w_down):
    for l in range(DEPTH):
        h = rms_norm(x, ffn1_pre_g[l])
        x = x + 0.5 * rms_norm(swiglu(h, ffn1_w_gate[l], ffn1_w_up[l], ffn1_w_down[l]), ffn1_post_g[l])
        h = rms_norm(x, mix_pre_g[l])
        m = token_mixer(h, w_in[l], b_gate[l], rpb[l], w_branch_a[l], w_branch_b[l], w_out[l])
        x = x + rms_norm(m, mix_post_g[l])
        h = rms_norm(x, ffn2_pre_g[l])
        x = x + 0.5 * rms_norm(swiglu(h, ffn2_w_gate[l], ffn2_w_up[l], ffn2_w_down[l]), ffn2_post_g[l])
    return x


def setup_inputs(seed: int = 0) -> dict:
    key = jax.random.key(seed)
    ks = jax.random.split(key, 20)
    f32 = jnp.float32

    def nrm(k, shape, scale):
        return jax.random.normal(k, shape, f32) * scale

    def gain(k):
        return 1.0 + 0.1 * jax.random.normal(k, (DEPTH, D_MODEL), f32)

    return {
        "x_prompt": jax.random.normal(ks[0], (BATCH, SEQ, D_MODEL), f32),
        "x_sample": jax.random.normal(ks[1], (DEC_BATCH, DEC_SEQ, D_MODEL), f32),
        "ffn1_pre_g": gain(ks[2]),
        "ffn1_post_g": gain(ks[3]),
        "ffn1_w_gate": nrm(ks[4], (DEPTH, D_MODEL, D_FF), D_MODEL ** -0.5),
        "ffn1_w_up": nrm(ks[5], (DEPTH, D_MODEL, D_FF), D_MODEL ** -0.5),
        "ffn1_w_down": nrm(ks[6], (DEPTH, D_FF, D_MODEL), D_FF ** -0.5),
        "mix_pre_g": gain(ks[7]),
        "mix_post_g": gain(ks[8]),
        "w_in": nrm(ks[9], (DEPTH, D_MODEL, IN_WIDTH), D_MODEL ** -0.5),
        "b_gate": nrm(ks[10], (DEPTH, GATE_WIDTH), 0.1),
        "rpb": nrm(ks[11], (DEPTH, B_HEADS, 2 * NA_ROWS - 1, 2 * NA_COLS - 1), 0.5),
        "w_branch_a": nrm(ks[12], (DEPTH, A_OUT, D_MODEL), A_OUT ** -0.5),
        "w_branch_b": nrm(ks[13], (DEPTH, B_WIDTH, D_MODEL), B_WIDTH ** -0.5),
        "w_out": nrm(ks[14], (DEPTH, D_MODEL, D_MODEL), D_MODEL ** -0.5),
        "ffn2_pre_g": gain(ks[15]),
        "ffn2_post_g": gain(ks[16]),
        "ffn2_w_gate": nrm(ks[17], (DEPTH, D_MODEL, D_FF), D_MODEL ** -0.5),
        "ffn2_w_up": nrm(ks[18], (DEPTH, D_MODEL, D_FF), D_MODEL ** -0.5),
        "ffn2_w_down": nrm(ks[19], (DEPTH, D_FF, D_MODEL), D_FF ** -0.5),
    }


def reference(x_prompt, x_sample, ffn1_pre_g, ffn1_post_g, ffn1_w_gate, ffn1_w_up, ffn1_w_down,
              mix_pre_g, mix_post_g, w_in, b_gate, rpb, w_branch_a, w_branch_b, w_out,
              ffn2_pre_g, ffn2_post_g, ffn2_w_gate, ffn2_w_up, ffn2_w_down):
    y_prompt = encoder_stack(x_prompt, ffn1_pre_g, ffn1_post_g, ffn1_w_gate, ffn1_w_up, ffn1_w_down,
                             mix_pre_g, mix_post_g, w_in, b_gate, rpb, w_branch_a, w_branch_b, w_out,
                             ffn2_pre_g, ffn2_post_g, ffn2_w_gate, ffn2_w_up, ffn2_w_down)
    y_sample = encoder_stack(x_sample, ffn1_pre_g, ffn1_post_g, ffn1_w_gate, ffn1_w_up, ffn1_w_down,
                             mix_pre_g, mix_post_g, w_in, b_gate, rpb, w_branch_a, w_branch_b, w_out,
                             ffn2_pre_g, ffn2_post_g, ffn2_w_gate, ffn2_w_up, ffn2_w_down)
    return (y_prompt, y_sample)
```

```python
import functools

import jax
import jax.numpy as jnp
import numpy as np
from jax import lax
from jax.experimental import pallas as pl
from jax.experimental.pallas import tpu as pltpu

D_MODEL = 1024
D_FF = 2816
HEAD_DIM = 64
DILATIONS = (1, 4, 16)
BAND_RADIUS = 64
A_GROUP_HEADS = 4
A_GROUP_WIDTH = A_GROUP_HEADS * HEAD_DIM
A_WIDTH = 3 * A_GROUP_WIDTH
B_HEADS = 8
B_WIDTH = B_HEADS * HEAD_DIM
GRID_W = 64
NA_ROWS = 8
NA_COLS = 16
ROPE_THETA = 500000.0
ROT_DIM = HEAD_DIM // 4
RMS_EPS = 1e-6
NEG = -1e30

LANES = 128
VMEM_LIMIT = 56 * 1024 * 1024

TOKEN_TILE = 1024
FFN_TILE = 512
FF_CHUNKS = ((0, 1024), (1024, 1024), (2048, 768))
BAND_Q_BLOCK = 512
BAND_Q_TILE = 128
B_ROWS_PER_STEP = 8

BF16 = jnp.bfloat16
F32 = jnp.float32


def _rms(x, g):
    return x * lax.rsqrt(jnp.mean(x * x, axis=-1, keepdims=True) + RMS_EPS) * g


def _const_spec(shape):
    zeros = (0,) * len(shape)
    return pl.BlockSpec(shape, lambda *_: zeros, pipeline_mode=pl.Buffered(1))


def _params():
    return pltpu.CompilerParams(vmem_limit_bytes=VMEM_LIMIT)


def _ffn_kernel(x_ref, pre_ref, post_ref, wg_ref, wu_ref, wd_ref, o_ref):
    x = x_ref[...]
    h = _rms(x, pre_ref[...]).astype(BF16)
    y = None
    for start, size in FF_CHUNKS:
        g = jnp.dot(h, wg_ref[:, start:start + size], preferred_element_type=F32)
        u = jnp.dot(h, wu_ref[:, start:start + size], preferred_element_type=F32)
        a = (g * jax.nn.sigmoid(g) * u).astype(BF16)
        part = jnp.dot(a, wd_ref[start:start + size, :], preferred_element_type=F32)
        y = part if y is None else y + part
    o_ref[...] = x + 0.5 * _rms(y, post_ref[...])


def _ffn(x2d, pre_g, post_g, wg, wu, wd):
    m = x2d.shape[0]
    return pl.pallas_call(
        _ffn_kernel,
        grid=(m // FFN_TILE,),
        in_specs=[
            pl.BlockSpec((FFN_TILE, D_MODEL), lambda i: (i, 0)),
            _const_spec((1, D_MODEL)),
            _const_spec((1, D_MODEL)),
            _const_spec((D_MODEL, D_FF)),
            _const_spec((D_MODEL, D_FF)),
            _const_spec((D_FF, D_MODEL)),
        ],
        out_specs=pl.BlockSpec((FFN_TILE, D_MODEL), lambda i: (i, 0)),
        out_shape=jax.ShapeDtypeStruct((m, D_MODEL), F32),
        compiler_params=_params(),
    )(x2d, pre_g, post_g, wg, wu, wd)


def _inproj_kernel(x_ref, pre_ref, w_ref, cos_ref, sin_lo_ref, sin_hi_ref,
                   a0_ref, a1_ref, a2_ref, b_ref, slab_ref):
    h = _rms(x_ref[0], pre_ref[...]).astype(BF16)
    cos = cos_ref[...]
    sin_lo = sin_lo_ref[...]
    sin_hi = sin_hi_ref[...]
    n_slabs = A_WIDTH // LANES
    n_rot_slabs = 2 * A_GROUP_WIDTH // LANES
    for g, (d, out_ref) in enumerate(zip(DILATIONS, (a0_ref, a1_ref, a2_ref))):
        res = jnp.dot(h, w_ref[:, g * A_WIDTH:(g + 1) * A_WIDTH], preferred_element_type=F32)
        n = TOKEN_TILE // d
        for c in range(n_slabs):
            xs = res[:, c * LANES:(c + 1) * LANES]
            if c < n_rot_slabs:
                xs = (xs * cos + pltpu.roll(xs, ROT_DIM // 2, 1) * sin_hi
                      + pltpu.roll(xs, LANES - ROT_DIM // 2, 1) * sin_lo)
            if d == 1:
                out_ref[0, 0, :, c * LANES:(c + 1) * LANES] = xs.astype(BF16)
            else:
                slab_ref[c] = xs
        if d > 1:
            for c in range(n_slabs):
                for r in range(d):
                    rows = slab_ref[c, pl.ds(r, n, stride=d), :]
                    out_ref[0, r, :, c * LANES:(c + 1) * LANES] = rows.astype(BF16)
    res = jnp.dot(h, w_ref[:, 3 * A_WIDTH:], preferred_element_type=F32)
    b_ref[0] = res.astype(BF16)


def _inproj(x, pre_g, w_proj, cos_t, sin_lo_t, sin_hi_t):
    bsz, seq, _ = x.shape
    nblk = seq // TOKEN_TILE
    tab_spec = pl.BlockSpec((TOKEN_TILE, LANES), lambda b, i: (i, 0))
    out_shapes = [jax.ShapeDtypeStruct((bsz, d, seq // d, A_WIDTH), BF16) for d in DILATIONS]
    out_specs = [pl.BlockSpec((1, d, TOKEN_TILE // d, A_WIDTH), lambda b, i: (b, 0, i, 0)) for d in DILATIONS]
    out_shapes.append(jax.ShapeDtypeStruct((bsz, seq, 3 * B_WIDTH), BF16))
    out_specs.append(pl.BlockSpec((1, TOKEN_TILE, 3 * B_WIDTH), lambda b, i: (b, i, 0)))
    return pl.pallas_call(
        _inproj_kernel,
        grid=(bsz, nblk),
        in_specs=[
            pl.BlockSpec((1, TOKEN_TILE, D_MODEL), lambda b, i: (b, i, 0)),
            _const_spec((1, D_MODEL)),
            _const_spec((D_MODEL, 3 * A_WIDTH + 3 * B_WIDTH)),
            tab_spec, tab_spec, tab_spec,
        ],
        out_specs=out_specs,
        out_shape=out_shapes,
        scratch_shapes=[pltpu.VMEM((A_WIDTH // LANES, TOKEN_TILE, LANES), F32)],
        compiler_params=_params(),
    )(x, pre_g, w_proj, cos_t, sin_lo_t, sin_hi_t)


def _head_select(stacked, rows, n_heads, width):
    lane_head = lax.broadcasted_iota(jnp.int32, (rows, width), 1) // HEAD_DIM
    out = stacked[0:rows]
    for hh in range(1, n_heads):
        out = jnp.where(lane_head == hh, stacked[hh * rows:(hh + 1) * rows], out)
    return out


def _head_stack(q, n_heads):
    rows, width = q.shape
    lane_head = lax.broadcasted_iota(jnp.int32, (rows, width), 1) // HEAD_DIM
    zero = jnp.zeros_like(q)
    return jnp.concatenate([jnp.where(lane_head == hh, q, zero) for hh in range(n_heads)], axis=0)


def _attn_a_kernel(q_ref, kp_ref, kc_ref, kn_ref, vp_ref, vc_ref, vn_ref, o_ref, lse_ref, k_ext, v_ext,
                   *, seq_len, q_block):
    i = pl.program_id(1)
    r = BAND_RADIUS
    k_ext[0:r] = kp_ref[0]
    k_ext[r:r + q_block] = kc_ref[0]
    k_ext[r + q_block:] = kn_ref[0]
    v_ext[0:r] = vp_ref[0]
    v_ext[r:r + q_block] = vc_ref[0]
    v_ext[r + q_block:] = vn_ref[0]
    n_keys = BAND_Q_TILE + 2 * r
    rows = A_GROUP_HEADS * BAND_Q_TILE
    q_off = lax.broadcasted_iota(jnp.int32, (rows, n_keys), 0) % BAND_Q_TILE
    col = lax.broadcasted_iota(jnp.int32, (rows, n_keys), 1)
    band = (col >= q_off) & (col <= q_off + 2 * r)
    for j in range(q_block // BAND_Q_TILE):
        base = i * q_block + j * BAND_Q_TILE - r
        valid = band & (col >= -base) & (col < seq_len - base)
        q = _head_stack(q_ref[0, j * BAND_Q_TILE:(j + 1) * BAND_Q_TILE, :], A_GROUP_HEADS)
        k = k_ext[j * BAND_Q_TILE:j * BAND_Q_TILE + n_keys, :]
        v = v_ext[j * BAND_Q_TILE:j * BAND_Q_TILE + n_keys, :]
        s = lax.dot_general(q, k, (((1,), (1,)), ((), ())), preferred_element_type=F32)
        s = jnp.where(valid, s, NEG)
        m = jnp.max(s, axis=-1, keepdims=True)
        p = jnp.exp(s - m)
        l = jnp.sum(p, axis=-1, keepdims=True)
        pn = (p / l).astype(BF16)
        o = jnp.dot(pn, v, preferred_element_type=F32)
        lse = jnp.broadcast_to(m + jnp.log(l), o.shape)
        sl = slice(j * BAND_Q_TILE, (j + 1) * BAND_Q_TILE)
        o_ref[0, sl, :] = _head_select(o, BAND_Q_TILE, A_GROUP_HEADS, A_GROUP_WIDTH)
        lse_ref[0, sl, :] = _head_select(lse, BAND_Q_TILE, A_GROUP_HEADS, A_GROUP_WIDTH)


def _attn_a(qkv):
    n_seq, seq_len, _ = qkv.shape
    q_block = min(BAND_Q_BLOCK, seq_len)
    halo_per_block = q_block // BAND_RADIUS
    n_halo = seq_len // BAND_RADIUS

    def cur(col):
        return pl.BlockSpec((1, q_block, A_GROUP_WIDTH), lambda s, i: (s, i, col))

    def prev(col):
        return pl.BlockSpec((1, BAND_RADIUS, A_GROUP_WIDTH),
                            lambda s, i: (s, jnp.maximum(i * halo_per_block - 1, 0), col))

    def nxt(col):
        return pl.BlockSpec((1, BAND_RADIUS, A_GROUP_WIDTH),
                            lambda s, i: (s, jnp.minimum((i + 1) * halo_per_block, n_halo - 1), col))

    out_spec = pl.BlockSpec((1, q_block, A_GROUP_WIDTH), lambda s, i: (s, i, 0))
    out_shape = jax.ShapeDtypeStruct((n_seq, seq_len, A_GROUP_WIDTH), F32)
    ext = pltpu.VMEM((q_block + 2 * BAND_RADIUS, A_GROUP_WIDTH), BF16)
    return pl.pallas_call(
        functools.partial(_attn_a_kernel, seq_len=seq_len, q_block=q_block),
        grid=(n_seq, seq_len // q_block),
        in_specs=[cur(0), prev(1), cur(1), nxt(1), prev(2), cur(2), nxt(2)],
        out_specs=[out_spec, out_spec],
        out_shape=[out_shape, out_shape],
        scratch_shapes=[ext, ext],
        compiler_params=_params(),
    )(qkv, qkv, qkv, qkv, qkv, qkv, qkv)


def _attn_b_kernel(q_ref, kp_ref, kc_ref, kn_ref, vp_ref, vc_ref, vn_ref, bias_ref, o_ref, k_ext, v_ext):
    i = pl.program_id(1)
    last = pl.num_programs(1) - 1
    halo = (NA_ROWS // 2) * GRID_W
    step_tokens = B_ROWS_PER_STEP * GRID_W
    win = NA_ROWS * GRID_W
    k_ext[0:halo] = kp_ref[0]
    k_ext[halo:halo + step_tokens] = kc_ref[0]
    k_ext[halo + step_tokens:] = kn_ref[0]
    v_ext[0:halo] = vp_ref[0]
    v_ext[halo:halo + step_tokens] = vc_ref[0]
    v_ext[halo + step_tokens:] = vn_ref[0]
    lo = jnp.where(i == 0, NA_ROWS // 2, 0)
    hi = jnp.where(i == last, NA_ROWS // 2, B_ROWS_PER_STEP - 1)
    for rr in range(B_ROWS_PER_STEP):
        w_row = jnp.clip(rr, lo, hi)
        start = pl.multiple_of(w_row * GRID_W, GRID_W)
        delta = w_row - rr + (NA_ROWS // 2 - 1)
        q = _head_stack(q_ref[0, rr * GRID_W:(rr + 1) * GRID_W, :], B_HEADS)
        k = k_ext[pl.ds(start, win), :]
        v = v_ext[pl.ds(start, win), :]
        s = lax.dot_general(q, k, (((1,), (1,)), ((), ())), preferred_element_type=F32)
        s = s + bias_ref[delta]
        m = jnp.max(s, axis=-1, keepdims=True)
        p = jnp.exp(s - m)
        l = jnp.sum(p, axis=-1, keepdims=True)
        pn = (p / l).astype(BF16)
        o = jnp.dot(pn, v, preferred_element_type=F32)
        o_ref[0, rr * GRID_W:(rr + 1) * GRID_W, :] = _head_select(o, GRID_W, B_HEADS, B_WIDTH).astype(BF16)


def _attn_b(qkv, bias):
    bsz, seq, _ = qkv.shape
    step_tokens = B_ROWS_PER_STEP * GRID_W
    halo = (NA_ROWS // 2) * GRID_W
    per = step_tokens // halo
    n_halo = seq // halo

    def cur(col):
        return pl.BlockSpec((1, step_tokens, B_WIDTH), lambda b, i: (b, i, col))

    def prev(col):
        return pl.BlockSpec((1, halo, B_WIDTH), lambda b, i: (b, jnp.maximum(i * per - 1, 0), col))

    def nxt(col):
        return pl.BlockSpec((1, halo, B_WIDTH), lambda b, i: (b, jnp.minimum((i + 1) * per, n_halo - 1), col))

    ext = pltpu.VMEM((step_tokens + 2 * halo, B_WIDTH), BF16)
    return pl.pallas_call(
        _attn_b_kernel,
        grid=(bsz, seq // step_tokens),
        in_specs=[cur(0), prev(1), cur(1), nxt(1), prev(2), cur(2), nxt(2),
                  _const_spec(bias.shape)],
        out_specs=pl.BlockSpec((1, step_tokens, B_WIDTH), lambda b, i: (b, i, 0)),
        out_shape=jax.ShapeDtypeStruct((bsz, seq, B_WIDTH), BF16),
        scratch_shapes=[ext, ext],
        compiler_params=_params(),
    )(qkv, qkv, qkv, qkv, qkv, qkv, qkv, bias)


MIX_SUB = 256


def _mixout_kernel(x_ref, o0_ref, l0_ref, o1_ref, l1_ref, o2_ref, l2_ref, ob_ref,
                   pre_ref, post_ref, wgate_ref, bgate_ref, wa_ref, wb_ref, wout_ref,
                   y_ref, nat_ref):
    n_slabs = A_GROUP_WIDTH // LANES
    for k, (d, src) in enumerate(((DILATIONS[1], o1_ref), (DILATIONS[1], l1_ref),
                                  (DILATIONS[2], o2_ref), (DILATIONS[2], l2_ref))):
        n = TOKEN_TILE // d
        for c in range(n_slabs):
            for r in range(d):
                nat_ref[k * n_slabs + c, pl.ds(r, n, stride=d), :] = src[0, r, :, c * LANES:(c + 1) * LANES]

    def nat(k, sl):
        return jnp.concatenate([nat_ref[k * n_slabs + c, sl, :] for c in range(n_slabs)], axis=-1)

    for t in range(TOKEN_TILE // MIX_SUB):
        sl = slice(t * MIX_SUB, (t + 1) * MIX_SUB)
        x = x_ref[0, sl, :]
        h = _rms(x, pre_ref[...]).astype(BF16)
        gates = jax.nn.sigmoid(jnp.dot(h, wgate_ref[...], preferred_element_type=F32) + bgate_ref[...])
        o0, l0 = o0_ref[0, 0, sl, :], l0_ref[0, 0, sl, :]
        o1, l1, o2, l2 = nat(0, sl), nat(1, sl), nat(2, sl), nat(3, sl)
        mx = jnp.maximum(jnp.maximum(l0, l1), l2)
        e0, e1, e2 = jnp.exp(l0 - mx), jnp.exp(l1 - mx), jnp.exp(l2 - mx)
        oa = (e0 * o0 + e1 * o1 + e2 * o2) / (e0 + e1 + e2)
        ya = jnp.dot(oa.astype(BF16), wa_ref[...], preferred_element_type=F32)
        yb = jnp.dot(ob_ref[0, sl, :], wb_ref[...], preferred_element_type=F32)
        mixed = (gates[:, :D_MODEL] * ya + gates[:, D_MODEL:] * yb).astype(BF16)
        m = jnp.dot(mixed, wout_ref[...], preferred_element_type=F32)
        y_ref[0, sl, :] = x + _rms(m, post_ref[...])


def _mixout(x, oas, lses, ob, pre_g, post_g, w_gate, b_gate, w_a, w_b, w_out):
    bsz, seq, _ = x.shape
    nblk = seq // TOKEN_TILE
    tile_spec = pl.BlockSpec((1, TOKEN_TILE, D_MODEL), lambda b, i: (b, i, 0))
    in_specs = [tile_spec]
    operands = [x]
    for d, o, l in zip(DILATIONS, oas, lses):
        spec = pl.BlockSpec((1, d, TOKEN_TILE // d, A_GROUP_WIDTH), lambda b, i: (b, 0, i, 0))
        in_specs += [spec, spec]
        operands += [o, l]
    in_specs.append(pl.BlockSpec((1, TOKEN_TILE, B_WIDTH), lambda b, i: (b, i, 0)))
    operands.append(ob)
    for w in (pre_g, post_g, w_gate, b_gate, w_a, w_b, w_out):
        in_specs.append(_const_spec(w.shape))
        operands.append(w)
    return pl.pallas_call(
        _mixout_kernel,
        grid=(bsz, nblk),
        in_specs=in_specs,
        out_specs=tile_spec,
        out_shape=jax.ShapeDtypeStruct((bsz, seq, D_MODEL), F32),
        scratch_shapes=[pltpu.VMEM((4 * A_GROUP_WIDTH // LANES, TOKEN_TILE, LANES), F32)],
        compiler_params=_params(),
    )(*operands)


def _rotary_tables(seq):
    half = ROT_DIM // 2
    inv_freq = ROPE_THETA ** (-jnp.arange(0, ROT_DIM, 2, dtype=F32) / ROT_DIM)
    ang = jnp.arange(seq, dtype=F32)[:, None] * inv_freq[None, :]
    cos, sin = jnp.cos(ang), jnp.sin(ang)
    ones = jnp.ones((seq, HEAD_DIM - ROT_DIM), F32)
    zeros = jnp.zeros((seq, HEAD_DIM - ROT_DIM), F32)
    zh = jnp.zeros((seq, half), F32)
    cos_h = jnp.concatenate([cos, cos, ones], axis=1)
    sin_hi_h = jnp.concatenate([zh, sin, zeros], axis=1)
    sin_lo_h = jnp.concatenate([-sin, zh, zeros], axis=1)
    reps = LANES // HEAD_DIM
    return tuple(jnp.tile(t, (1, reps)) for t in (cos_h, sin_lo_h, sin_hi_h))


def _neighbourhood_bias(rpb):
    cj = np.arange(GRID_W)
    cs = np.clip(cj - NA_COLS // 2, 0, GRID_W - NA_COLS)
    col_valid = (cj[None, :] >= cs[:, None]) & (cj[None, :] < cs[:, None] + NA_COLS)
    dc_idx = np.clip(cj[None, :] - cj[:, None] + (NA_COLS - 1), 0, 2 * NA_COLS - 2)
    a = np.arange(NA_ROWS)
    tables = []
    for offset in range(-(NA_ROWS - 1), 1):
        dr_idx = offset + a + (NA_ROWS - 1)
        bias = rpb[:, dr_idx[None, :, None], dc_idx[:, None, :]]
        bias = jnp.where(col_valid[None, :, None, :], bias.astype(F32), NEG)
        tables.append(bias.reshape(B_HEADS * GRID_W, NA_ROWS * GRID_W))
    return jnp.stack(tables, axis=0)


def _split_w_in(w_in):
    scale = HEAD_DIM ** -0.5
    o = 0
    qa = w_in[:, o:o + A_WIDTH] * scale; o += A_WIDTH
    ka = w_in[:, o:o + A_WIDTH]; o += A_WIDTH
    va = w_in[:, o:o + A_WIDTH]; o += A_WIDTH
    qb = w_in[:, o:o + B_WIDTH] * scale; o += B_WIDTH
    kb = w_in[:, o:o + B_WIDTH]; o += B_WIDTH
    vb = w_in[:, o:o + B_WIDTH]; o += B_WIDTH
    w_gate = w_in[:, o:]
    cols = []
    for g in range(len(DILATIONS)):
        sl = slice(g * A_GROUP_WIDTH, (g + 1) * A_GROUP_WIDTH)
        cols += [qa[:, sl], ka[:, sl], va[:, sl]]
    cols += [qb, kb, vb]
    return jnp.concatenate(cols, axis=1).astype(BF16), w_gate.astype(BF16)


def _encoder(x, p, tables):
    bsz, seq, _ = x.shape
    row = lambda v: v.reshape(1, -1)
    x1 = _ffn(x.reshape(bsz * seq, D_MODEL), row(p["ffn1_pre_g"]), row(p["ffn1_post_g"]),
              p["ffn1_wg"], p["ffn1_wu"], p["ffn1_wd"]).reshape(bsz, seq, D_MODEL)
    cos_t, sin_lo_t, sin_hi_t = tables
    a0, a1, a2, qkvb = _inproj(x1, row(p["mix_pre_g"]), p["w_proj"], cos_t, sin_lo_t, sin_hi_t)
    oas, lses = [], []
    for d, qkv in zip(DILATIONS, (a0, a1, a2)):
        o, lse = _attn_a(qkv.reshape(bsz * d, seq // d, A_WIDTH))
        oas.append(o.reshape(bsz, d, seq // d, A_GROUP_WIDTH))
        lses.append(lse.reshape(bsz, d, seq // d, A_GROUP_WIDTH))
    ob = _attn_b(qkvb, p["bias_b"])
    x2 = _mixout(x1, oas, lses, ob, row(p["mix_pre_g"]), row(p["mix_post_g"]), p["w_gate"], row(p["b_gate"]),
                 p["w_a"], p["w_b"], p["w_out"])
    x3 = _ffn(x2.reshape(bsz * seq, D_MODEL), row(p["ffn2_pre_g"]), row(p["ffn2_post_g"]),
              p["ffn2_wg"], p["ffn2_wu"], p["ffn2_wd"])
    return x3.reshape(bsz, seq, D_MODEL)


def kernel(x_prompt, x_sample, ffn1_pre_g, ffn1_post_g, ffn1_w_gate, ffn1_w_up, ffn1_w_down, mix_pre_g, mix_post_g,
           w_in, b_gate, rpb, w_branch_a, w_branch_b, w_out, ffn2_pre_g, ffn2_post_g, ffn2_w_gate, ffn2_w_up,
           ffn2_w_down):
    w_proj, w_gate = _split_w_in(w_in[0])
    p = {
        "ffn1_pre_g": ffn1_pre_g[0], "ffn1_post_g": ffn1_post_g[0],
        "ffn1_wg": ffn1_w_gate[0].astype(BF16), "ffn1_wu": ffn1_w_up[0].astype(BF16),
        "ffn1_wd": ffn1_w_down[0].astype(BF16),
        "mix_pre_g": mix_pre_g[0], "mix_post_g": mix_post_g[0],
        "w_proj": w_proj, "w_gate": w_gate, "b_gate": b_gate[0],
        "bias_b": _neighbourhood_bias(rpb[0]),
        "w_a": w_branch_a[0].astype(BF16), "w_b": w_branch_b[0].astype(BF16), "w_out": w_out[0].astype(BF16),
        "ffn2_pre_g": ffn2_pre_g[0], "ffn2_post_g": ffn2_post_g[0],
        "ffn2_wg": ffn2_w_gate[0].astype(BF16), "ffn2_wu": ffn2_w_up[0].astype(BF16),
        "ffn2_wd": ffn2_w_down[0].astype(BF16),
    }
    tables = _rotary_tables(max(x_prompt.shape[1], x_sample.shape[1]))
    return _encoder(x_prompt, p, tables), _encoder(x_sample, p, tables)
```

```python
import functools

import jax
import jax.numpy as jnp
import numpy as np
from jax import lax
from jax.experimental import pallas as pl
from jax.experimental.pallas import tpu as pltpu

D_MODEL = 1024
D_FF = 2816
HEAD_DIM = 64
DILATIONS = (1, 4, 16)
BAND_RADIUS = 64
A_GROUP_HEADS = 4
A_GROUP_WIDTH = A_GROUP_HEADS * HEAD_DIM
A_WIDTH = 3 * A_GROUP_WIDTH
B_HEADS = 8
B_WIDTH = B_HEADS * HEAD_DIM
GRID_W = 64
NA_ROWS = 8
NA_COLS = 16
ROPE_THETA = 500000.0
ROT_DIM = HEAD_DIM // 4
RMS_EPS = 1e-6
NEG = -1e30

LANES = 128
VMEM_LIMIT = 56 * 1024 * 1024

TOKEN_TILE = 1024
FFN_TILE = 512
FF_CHUNKS = ((0, 1024), (1024, 1024), (2048, 768))
BAND_Q_BLOCK = 512
BAND_Q_TILE = 128
B_ROWS_PER_STEP = 8

BF16 = jnp.bfloat16
F32 = jnp.float32


def _rms(x, g):
    return x * lax.rsqrt(jnp.mean(x * x, axis=-1, keepdims=True) + RMS_EPS) * g


def _const_spec(shape):
    zeros = (0,) * len(shape)
    return pl.BlockSpec(shape, lambda *_: zeros, pipeline_mode=pl.Buffered(1))


def _params():
    return pltpu.CompilerParams(vmem_limit_bytes=VMEM_LIMIT)


def _ffn_kernel(x_ref, pre_ref, post_ref, wg_ref, wu_ref, wd_ref, o_ref):
    x = x_ref[...]
    h = _rms(x, pre_ref[...]).astype(BF16)
    y = None
    for start, size in FF_CHUNKS:
        g = jnp.dot(h, wg_ref[:, start:start + size], preferred_element_type=F32)
        u = jnp.dot(h, wu_ref[:, start:start + size], preferred_element_type=F32)
        a = (g * jax.nn.sigmoid(g) * u).astype(BF16)
        part = jnp.dot(a, wd_ref[start:start + size, :], preferred_element_type=F32)
        y = part if y is None else y + part
    o_ref[...] = x + 0.5 * _rms(y, post_ref[...])


def _ffn(x2d, pre_g, post_g, wg, wu, wd):
    m = x2d.shape[0]
    return pl.pallas_call(
        _ffn_kernel,
        grid=(m // FFN_TILE,),
        in_specs=[
            pl.BlockSpec((FFN_TILE, D_MODEL), lambda i: (i, 0)),
            _const_spec((1, D_MODEL)),
            _const_spec((1, D_MODEL)),
            _const_spec((D_MODEL, D_FF)),
            _const_spec((D_MODEL, D_FF)),
            _const_spec((D_FF, D_MODEL)),
        ],
        out_specs=pl.BlockSpec((FFN_TILE, D_MODEL), lambda i: (i, 0)),
        out_shape=jax.ShapeDtypeStruct((m, D_MODEL), F32),
        compiler_params=_params(),
    )(x2d, pre_g, post_g, wg, wu, wd)


def _inproj_kernel(x_ref, pre_ref, w_ref, cos_ref, sin_lo_ref, sin_hi_ref,
                   a0_ref, a1_ref, a2_ref, b_ref, slab_ref):
    h = _rms(x_ref[0], pre_ref[...]).astype(BF16)
    cos = cos_ref[...]
    sin_lo = sin_lo_ref[...]
    sin_hi = sin_hi_ref[...]
    n_slabs = A_WIDTH // LANES
    n_rot_slabs = 2 * A_GROUP_WIDTH // LANES
    for g, (d, out_ref) in enumerate(zip(DILATIONS, (a0_ref, a1_ref, a2_ref))):
        res = jnp.dot(h, w_ref[:, g * A_WIDTH:(g + 1) * A_WIDTH], preferred_element_type=F32)
        n = TOKEN_TILE // d
        for c in range(n_slabs):
            xs = res[:, c * LANES:(c + 1) * LANES]
            if c < n_rot_slabs:
                xs = (xs * cos + pltpu.roll(xs, ROT_DIM // 2, 1) * sin_hi
                      + pltpu.roll(xs, LANES - ROT_DIM // 2, 1) * sin_lo)
            if d == 1:
                out_ref[0, 0, :, c * LANES:(c + 1) * LANES] = xs.astype(BF16)
            else:
                slab_ref[c] = xs
        if d > 1:
            for c in range(n_slabs):
                for r in range(d):
                    rows = slab_ref[c, pl.ds(r, n, stride=d), :]
                    out_ref[0, r, :, c * LANES:(c + 1) * LANES] = rows.astype(BF16)
    res = jnp.dot(h, w_ref[:, 3 * A_WIDTH:], preferred_element_type=F32)
    b_ref[0] = res.astype(BF16)


def _inproj(x, pre_g, w_proj, cos_t, sin_lo_t, sin_hi_t):
    bsz, seq, _ = x.shape
    nblk = seq // TOKEN_TILE
    tab_spec = pl.BlockSpec((TOKEN_TILE, LANES), lambda b, i: (i, 0))
    out_shapes = [jax.ShapeDtypeStruct((bsz, d, seq // d, A_WIDTH), BF16) for d in DILATIONS]
    out_specs = [pl.BlockSpec((1, d, TOKEN_TILE // d, A_WIDTH), lambda b, i: (b, 0, i, 0)) for d in DILATIONS]
    out_shapes.append(jax.ShapeDtypeStruct((bsz, seq, 3 * B_WIDTH), BF16))
    out_specs.append(pl.BlockSpec((1, TOKEN_TILE, 3 * B_WIDTH), lambda b, i: (b, i, 0)))
    return pl.pallas_call(
        _inproj_kernel,
        grid=(bsz, nblk),
        in_specs=[
            pl.BlockSpec((1, TOKEN_TILE, D_MODEL), lambda b, i: (b, i, 0)),
            _const_spec((1, D_MODEL)),
            _const_spec((D_MODEL, 3 * A_WIDTH + 3 * B_WIDTH)),
            tab_spec, tab_spec, tab_spec,
        ],
        out_specs=out_specs,
        out_shape=out_shapes,
        scratch_shapes=[pltpu.VMEM((A_WIDTH // LANES, TOKEN_TILE, LANES), F32)],
        compiler_params=_params(),
    )(x, pre_g, w_proj, cos_t, sin_lo_t, sin_hi_t)


def _head_select(stacked, rows, n_heads, width):
    lane_head = lax.broadcasted_iota(jnp.int32, (rows, width), 1) // HEAD_DIM
    out = stacked[0:rows]
    for hh in range(1, n_heads):
        out = jnp.where(lane_head == hh, stacked[hh * rows:(hh + 1) * rows], out)
    return out


def _head_stack(q, n_heads):
    rows, width = q.shape
    lane_head = lax.broadcasted_iota(jnp.int32, (rows, width), 1) // HEAD_DIM
    zero = jnp.zeros_like(q)
    return jnp.concatenate([jnp.where(lane_head == hh, q, zero) for hh in range(n_heads)], axis=0)


def _attn_a_kernel(q_ref, kp_ref, kc_ref, kn_ref, vp_ref, vc_ref, vn_ref, o_ref, lse_ref, k_ext, v_ext,
                   *, seq_len, q_block):
    i = pl.program_id(1)
    r = BAND_RADIUS
    k_ext[0:r] = kp_ref[0]
    k_ext[r:r + q_block] = kc_ref[0]
    k_ext[r + q_block:] = kn_ref[0]
    v_ext[0:r] = vp_ref[0]
    v_ext[r:r + q_block] = vc_ref[0]
    v_ext[r + q_block:] = vn_ref[0]
    n_keys = BAND_Q_TILE + 2 * r
    rows = A_GROUP_HEADS * BAND_Q_TILE
    q_off = lax.broadcasted_iota(jnp.int32, (rows, n_keys), 0) % BAND_Q_TILE
    col = lax.broadcasted_iota(jnp.int32, (rows, n_keys), 1)
    band = (col >= q_off) & (col <= q_off + 2 * r)
    for j in range(q_block // BAND_Q_TILE):
        base = i * q_block + j * BAND_Q_TILE - r
        valid = band & (col >= -base) & (col < seq_len - base)
        q = _head_stack(q_ref[0, j * BAND_Q_TILE:(j + 1) * BAND_Q_TILE, :], A_GROUP_HEADS)
        k = k_ext[j * BAND_Q_TILE:j * BAND_Q_TILE + n_keys, :]
        v = v_ext[j * BAND_Q_TILE:j * BAND_Q_TILE + n_keys, :]
        s = lax.dot_general(q, k, (((1,), (1,)), ((), ())), preferred_element_type=F32)
        s = jnp.where(valid, s, NEG)
        m = jnp.max(s, axis=-1, keepdims=True)
        p = jnp.exp(s - m)
        l = jnp.sum(p, axis=-1, keepdims=True)
        pn = (p / l).astype(BF16)
        o = jnp.dot(pn, v, preferred_element_type=F32)
        lse = jnp.broadcast_to(m + jnp.log(l), o.shape)
        sl = slice(j * BAND_Q_TILE, (j + 1) * BAND_Q_TILE)
        o_ref[0, sl, :] = _head_select(o, BAND_Q_TILE, A_GROUP_HEADS, A_GROUP_WIDTH)
        lse_ref[0, sl, :] = _head_select(lse, BAND_Q_TILE, A_GROUP_HEADS, A_GROUP_WIDTH)


def _attn_a(qkv):
    n_seq, seq_len, _ = qkv.shape
    q_block = min(BAND_Q_BLOCK, seq_len)
    halo_per_block = q_block // BAND_RADIUS
    n_halo = seq_len // BAND_RADIUS

    def cur(col):
        return pl.BlockSpec((1, q_block, A_GROUP_WIDTH), lambda s, i: (s, i, col))

    def prev(col):
        return pl.BlockSpec((1, BAND_RADIUS, A_GROUP_WIDTH),
                            lambda s, i: (s, jnp.maximum(i * halo_per_block - 1, 0), col))

    def nxt(col):
        return pl.BlockSpec((1, BAND_RADIUS, A_GROUP_WIDTH),
                            lambda s, i: (s, jnp.minimum((i + 1) * halo_per_block, n_halo - 1), col))

    out_spec = pl.BlockSpec((1, q_block, A_GROUP_WIDTH), lambda s, i: (s, i, 0))
    out_shape = jax.ShapeDtypeStruct((n_seq, seq_len, A_GROUP_WIDTH), F32)
    ext = pltpu.VMEM((q_block + 2 * BAND_RADIUS, A_GROUP_WIDTH), BF16)
    return pl.pallas_call(
        functools.partial(_attn_a_kernel, seq_len=seq_len, q_block=q_block),
        grid=(n_seq, seq_len // q_block),
        in_specs=[cur(0), prev(1), cur(1), nxt(1), prev(2), cur(2), nxt(2)],
        out_specs=[out_spec, out_spec],
        out_shape=[out_shape, out_shape],
        scratch_shapes=[ext, ext],
        compiler_params=_params(),
    )(qkv, qkv, qkv, qkv, qkv, qkv, qkv)


def _attn_b_kernel(q_ref, kp_ref, kc_ref, kn_ref, vp_ref, vc_ref, vn_ref, bias_ref, o_ref, k_ext, v_ext):
    i = pl.program_id(1)
    last = pl.num_programs(1) - 1
    halo = (NA_ROWS // 2) * GRID_W
    step_tokens = B_ROWS_PER_STEP * GRID_W
    win = NA_ROWS * GRID_W
    k_ext[0:halo] = kp_ref[0]
    k_ext[halo:halo + step_tokens] = kc_ref[0]
    k_ext[halo + step_tokens:] = kn_ref[0]
    v_ext[0:halo] = vp_ref[0]
    v_ext[halo:halo + step_tokens] = vc_ref[0]
    v_ext[halo + step_tokens:] = vn_ref[0]
    lo = jnp.where(i == 0, NA_ROWS // 2, 0)
    hi = jnp.where(i == last, NA_ROWS // 2, B_ROWS_PER_STEP - 1)
    for rr in range(B_ROWS_PER_STEP):
        w_row = jnp.clip(rr, lo, hi)
        start = pl.multiple_of(w_row * GRID_W, GRID_W)
        delta = w_row - rr + (NA_ROWS // 2 - 1)
        q = _head_stack(q_ref[0, rr * GRID_W:(rr + 1) * GRID_W, :], B_HEADS)
        k = k_ext[pl.ds(start, win), :]
        v = v_ext[pl.ds(start, win), :]
        s = lax.dot_general(q, k, (((1,), (1,)), ((), ())), preferred_element_type=F32)
        s = s + bias_ref[delta]
        m = jnp.max(s, axis=-1, keepdims=True)
        p = jnp.exp(s - m)
        l = jnp.sum(p, axis=-1, keepdims=True)
        pn = (p / l).astype(BF16)
        o = jnp.dot(pn, v, preferred_element_type=F32)
        o_ref[0, rr * GRID_W:(rr + 1) * GRID_W, :] = _head_select(o, GRID_W, B_HEADS, B_WIDTH).astype(BF16)


def _attn_b(qkv, bias):
    bsz, seq, _ = qkv.shape
    step_tokens = B_ROWS_PER_STEP * GRID_W
    halo = (NA_ROWS // 2) * GRID_W
    per = step_tokens // halo
    n_halo = seq // halo

    def cur(col):
        return pl.BlockSpec((1, step_tokens, B_WIDTH), lambda b, i: (b, i, col))

    def prev(col):
        return pl.BlockSpec((1, halo, B_WIDTH), lambda b, i: (b, jnp.maximum(i * per - 1, 0), col))

    def nxt(col):
        return pl.BlockSpec((1, halo, B_WIDTH), lambda b, i: (b, jnp.minimum((i + 1) * per, n_halo - 1), col))

    ext = pltpu.VMEM((step_tokens + 2 * halo, B_WIDTH), BF16)
    return pl.pallas_call(
        _attn_b_kernel,
        grid=(bsz, seq // step_tokens),
        in_specs=[cur(0), prev(1), cur(1), nxt(1), prev(2), cur(2), nxt(2),
                  _const_spec(bias.shape)],
        out_specs=pl.BlockSpec((1, step_tokens, B_WIDTH), lambda b, i: (b, i, 0)),
        out_shape=jax.ShapeDtypeStruct((bsz, seq, B_WIDTH), BF16),
        scratch_shapes=[ext, ext],
        compiler_params=_params(),
    )(qkv, qkv, qkv, qkv, qkv, qkv, qkv, bias)


MIX_SUB = 256


def _mixout_kernel(x_ref, o0_ref, l0_ref, o1_ref, l1_ref, o2_ref, l2_ref, ob_ref,
                   pre_ref, post_ref, wgate_ref, bgate_ref, wa_ref, wb_ref, wout_ref,
                   y_ref, nat_ref):
    n_slabs = A_GROUP_WIDTH // LANES
    for k, (d, src) in enumerate(((DILATIONS[1], o1_ref), (DILATIONS[1], l1_ref),
                                  (DILATIONS[2], o2_ref), (DILATIONS[2], l2_ref))):
        n = TOKEN_TILE // d
        for c in range(n_slabs):
            for r in range(d):
                nat_ref[k * n_slabs + c, pl.ds(r, n, stride=d), :] = src[0, r, :, c * LANES:(c + 1) * LANES]

    def nat(k, sl):
        return jnp.concatenate([nat_ref[k * n_slabs + c, sl, :] for c in range(n_slabs)], axis=-1)

    for t in range(TOKEN_TILE // MIX_SUB):
        sl = slice(t * MIX_SUB, (t + 1) * MIX_SUB)
        x = x_ref[0, sl, :]
        h = _rms(x, pre_ref[...]).astype(BF16)
        gates = jax.nn.sigmoid(jnp.dot(h, wgate_ref[...], preferred_element_type=F32) + bgate_ref[...])
        o0, l0 = o0_ref[0, 0, sl, :], l0_ref[0, 0, sl, :]
        o1, l1, o2, l2 = nat(0, sl), nat(1, sl), nat(2, sl), nat(3, sl)
        mx = jnp.maximum(jnp.maximum(l0, l1), l2)
        e0, e1, e2 = jnp.exp(l0 - mx), jnp.exp(l1 - mx), jnp.exp(l2 - mx)
        oa = (e0 * o0 + e1 * o1 + e2 * o2) / (e0 + e1 + e2)
        ya = jnp.dot(oa.astype(BF16), wa_ref[...], preferred_element_type=F32)
        yb = jnp.dot(ob_ref[0, sl, :], wb_ref[...], preferred_element_type=F32)
        mixed = (gates[:, :D_MODEL] * ya + gates[:, D_MODEL:] * yb).astype(BF16)
        m = jnp.dot(mixed, wout_ref[...], preferred_element_type=F32)
        y_ref[0, sl, :] = x + _rms(m, post_ref[...])


def _mixout(x, oas, lses, ob, pre_g, post_g, w_gate, b_gate, w_a, w_b, w_out):
    bsz, seq, _ = x.shape
    nblk = seq // TOKEN_TILE
    tile_spec = pl.BlockSpec((1, TOKEN_TILE, D_MODEL), lambda b, i: (b, i, 0))
    in_specs = [tile_spec]
    operands = [x]
    for d, o, l in zip(DILATIONS, oas, lses):
        spec = pl.BlockSpec((1, d, TOKEN_TILE // d, A_GROUP_WIDTH), lambda b, i: (b, 0, i, 0))
        in_specs += [spec, spec]
        operands += [o, l]
    in_specs.append(pl.BlockSpec((1, TOKEN_TILE, B_WIDTH), lambda b, i: (b, i, 0)))
    operands.append(ob)
    for w in (pre_g, post_g, w_gate, b_gate, w_a, w_b, w_out):
        in_specs.append(_const_spec(w.shape))
        operands.append(w)
    return pl.pallas_call(
        _mixout_kernel,
        grid=(bsz, nblk),
        in_specs=in_specs,
        out_specs=tile_spec,
        out_shape=jax.ShapeDtypeStruct((bsz, seq, D_MODEL), F32),
        scratch_shapes=[pltpu.VMEM((4 * A_GROUP_WIDTH // LANES, TOKEN_TILE, LANES), F32)],
        compiler_params=_params(),
    )(*operands)


def _rotary_tables(seq):
    half = ROT_DIM // 2
    inv_freq = ROPE_THETA ** (-jnp.arange(0, ROT_DIM, 2, dtype=F32) / ROT_DIM)
    ang = jnp.arange(seq, dtype=F32)[:, None] * inv_freq[None, :]
    cos, sin = jnp.cos(ang), jnp.sin(ang)
    ones = jnp.ones((seq, HEAD_DIM - ROT_DIM), F32)
    zeros = jnp.zeros((seq, HEAD_DIM - ROT_DIM), F32)
    zh = jnp.zeros((seq, half), F32)
    cos_h = jnp.concatenate([cos, cos, ones], axis=1)
    sin_hi_h = jnp.concatenate([zh, sin, zeros], axis=1)
    sin_lo_h = jnp.concatenate([-sin, zh, zeros], axis=1)
    reps = LANES // HEAD_DIM
    return tuple(jnp.tile(t, (1, reps)) for t in (cos_h, sin_lo_h, sin_hi_h))


def _neighbourhood_bias(rpb):
    cj = np.arange(GRID_W)
    cs = np.clip(cj - NA_COLS // 2, 0, GRID_W - NA_COLS)
    col_valid = (cj[None, :] >= cs[:, None]) & (cj[None, :] < cs[:, None] + NA_COLS)
    dc_idx = np.clip(cj[None, :] - cj[:, None] + (NA_COLS - 1), 0, 2 * NA_COLS - 2)
    n_dc = 2 * NA_COLS - 1
    onehot = (dc_idx[None] == np.arange(n_dc)[:, None, None]).astype(np.float32)
    base = jnp.einsum("hrd,dqk->hrqk", rpb.astype(F32), jnp.asarray(onehot), precision=lax.Precision.HIGHEST)
    base = jnp.where(col_valid[None, None], base, NEG)
    tables = []
    for first in range(NA_ROWS):
        t = base[:, first:first + NA_ROWS].transpose(0, 2, 1, 3)
        tables.append(t.reshape(B_HEADS * GRID_W, NA_ROWS * GRID_W))
    return jnp.stack(tables, axis=0)


def _split_w_in(w_in):
    scale = HEAD_DIM ** -0.5
    o = 0
    qa = w_in[:, o:o + A_WIDTH] * scale; o += A_WIDTH
    ka = w_in[:, o:o + A_WIDTH]; o += A_WIDTH
    va = w_in[:, o:o + A_WIDTH]; o += A_WIDTH
    qb = w_in[:, o:o + B_WIDTH] * scale; o += B_WIDTH
    kb = w_in[:, o:o + B_WIDTH]; o += B_WIDTH
    vb = w_in[:, o:o + B_WIDTH]; o += B_WIDTH
    w_gate = w_in[:, o:]
    cols = []
    for g in range(len(DILATIONS)):
        sl = slice(g * A_GROUP_WIDTH, (g + 1) * A_GROUP_WIDTH)
        cols += [qa[:, sl], ka[:, sl], va[:, sl]]
    cols += [qb, kb, vb]
    return jnp.concatenate(cols, axis=1).astype(BF16), w_gate.astype(BF16)


def _encoder(x, p, tables):
    bsz, seq, _ = x.shape
    row = lambda v: v.reshape(1, -1)
    x1 = _ffn(x.reshape(bsz * seq, D_MODEL), row(p["ffn1_pre_g"]), row(p["ffn1_post_g"]),
              p["ffn1_wg"], p["ffn1_wu"], p["ffn1_wd"]).reshape(bsz, seq, D_MODEL)
    cos_t, sin_lo_t, sin_hi_t = tables
    a0, a1, a2, qkvb = _inproj(x1, row(p["mix_pre_g"]), p["w_proj"], cos_t, sin_lo_t, sin_hi_t)
    oas, lses = [], []
    for d, qkv in zip(DILATIONS, (a0, a1, a2)):
        o, lse = _attn_a(qkv.reshape(bsz * d, seq // d, A_WIDTH))
        oas.append(o.reshape(bsz, d, seq // d, A_GROUP_WIDTH))
        lses.append(lse.reshape(bsz, d, seq // d, A_GROUP_WIDTH))
    ob = _attn_b(qkvb, p["bias_b"])
    x2 = _mixout(x1, oas, lses, ob, row(p["mix_pre_g"]), row(p["mix_post_g"]), p["w_gate"], row(p["b_gate"]),
                 p["w_a"], p["w_b"], p["w_out"])
    x3 = _ffn(x2.reshape(bsz * seq, D_MODEL), row(p["ffn2_pre_g"]), row(p["ffn2_post_g"]),
              p["ffn2_wg"], p["ffn2_wu"], p["ffn2_wd"])
    return x3.reshape(bsz, seq, D_MODEL)


def kernel(x_prompt, x_sample, ffn1_pre_g, ffn1_post_g, ffn1_w_gate, ffn1_w_up, ffn1_w_down, mix_pre_g, mix_post_g,
           w_in, b_gate, rpb, w_branch_a, w_branch_b, w_out, ffn2_pre_g, ffn2_post_g, ffn2_w_gate, ffn2_w_up,
           ffn2_w_down):
    w_proj, w_gate = _split_w_in(w_in[0])
    p = {
        "ffn1_pre_g": ffn1_pre_g[0], "ffn1_post_g": ffn1_post_g[0],
        "ffn1_wg": ffn1_w_gate[0].astype(BF16), "ffn1_wu": ffn1_w_up[0].astype(BF16),
        "ffn1_wd": ffn1_w_down[0].astype(BF16),
        "mix_pre_g": mix_pre_g[0], "mix_post_g": mix_post_g[0],
        "w_proj": w_proj, "w_gate": w_gate, "b_gate": b_gate[0],
        "bias_b": _neighbourhood_bias(rpb[0]),
        "w_a": w_branch_a[0].astype(BF16), "w_b": w_branch_b[0].astype(BF16), "w_out": w_out[0].astype(BF16),
        "ffn2_pre_g": ffn2_pre_g[0], "ffn2_post_g": ffn2_post_g[0],
        "ffn2_wg": ffn2_w_gate[0].astype(BF16), "ffn2_wu": ffn2_w_up[0].astype(BF16),
        "ffn2_wd": ffn2_w_down[0].astype(BF16),
    }
    tables = _rotary_tables(max(x_prompt.shape[1], x_sample.shape[1]))
    return _encoder(x_prompt, p, tables), _encoder(x_sample, p, tables)
```

```python
import functools

import jax
import jax.numpy as jnp
import numpy as np
from jax import lax
from jax.experimental import pallas as pl
from jax.experimental.pallas import tpu as pltpu

D_MODEL = 1024
D_FF = 2816
HEAD_DIM = 64
DILATIONS = (1, 4, 16)
BAND_RADIUS = 64
A_GROUP_HEADS = 4
A_GROUP_WIDTH = A_GROUP_HEADS * HEAD_DIM
A_WIDTH = 3 * A_GROUP_WIDTH
B_HEADS = 8
B_WIDTH = B_HEADS * HEAD_DIM
GRID_W = 64
NA_ROWS = 8
NA_COLS = 16
ROPE_THETA = 500000.0
ROT_DIM = HEAD_DIM // 4
RMS_EPS = 1e-6
NEG = -1e30

LANES = 128
VMEM_LIMIT = 56 * 1024 * 1024

TOKEN_TILE = 1024
FFN_TILE = 1024
FFN_SUB = 512
FF_CHUNKS = ((0, 1024), (1024, 1024), (2048, 768))
BAND_Q_BLOCK = 1024
BAND_Q_TILE = 128
B_ROWS_PER_STEP = 8
B_HALF_HEADS = 4
B_HALF_WIDTH = B_HALF_HEADS * HEAD_DIM

BF16 = jnp.bfloat16
F32 = jnp.float32


def _rms(x, g):
    return x * lax.rsqrt(jnp.mean(x * x, axis=-1, keepdims=True) + RMS_EPS) * g


def _const_spec(shape):
    zeros = (0,) * len(shape)
    return pl.BlockSpec(shape, lambda *_: zeros, pipeline_mode=pl.Buffered(1))


def _params():
    return pltpu.CompilerParams(vmem_limit_bytes=VMEM_LIMIT)


def _ffn_kernel(x_ref, pre_ref, post_ref, wg_ref, wu_ref, wd_ref, o_ref):
    n_sub = FFN_TILE // FFN_SUB

    def rows(t):
        return slice(t * FFN_SUB, (t + 1) * FFN_SUB)

    def pre(t):
        return _rms(x_ref[rows(t), :], pre_ref[...]).astype(BF16)

    def finish(t, y):
        o_ref[rows(t), :] = x_ref[rows(t), :] + 0.5 * _rms(y, post_ref[...])

    h = pre(0)
    y_prev = None
    for t in range(n_sub):
        y = None
        h_next = None
        for c, (start, size) in enumerate(FF_CHUNKS):
            g = jnp.dot(h, wg_ref[:, start:start + size], preferred_element_type=F32)
            u = jnp.dot(h, wu_ref[:, start:start + size], preferred_element_type=F32)
            a = (g * jax.nn.sigmoid(g) * u).astype(BF16)
            part = jnp.dot(a, wd_ref[start:start + size, :], preferred_element_type=F32)
            y = part if y is None else y + part
            if c == 0:
                if t + 1 < n_sub:
                    h_next = pre(t + 1)
                if t > 0:
                    finish(t - 1, y_prev)
        h, y_prev = h_next, y
    finish(n_sub - 1, y_prev)


def _ffn(x2d, pre_g, post_g, wg, wu, wd):
    m = x2d.shape[0]
    return pl.pallas_call(
        _ffn_kernel,
        grid=(m // FFN_TILE,),
        in_specs=[
            pl.BlockSpec((FFN_TILE, D_MODEL), lambda i: (i, 0)),
            _const_spec((1, D_MODEL)),
            _const_spec((1, D_MODEL)),
            _const_spec((D_MODEL, D_FF)),
            _const_spec((D_MODEL, D_FF)),
            _const_spec((D_FF, D_MODEL)),
        ],
        out_specs=pl.BlockSpec((FFN_TILE, D_MODEL), lambda i: (i, 0)),
        out_shape=jax.ShapeDtypeStruct((m, D_MODEL), F32),
        compiler_params=_params(),
    )(x2d, pre_g, post_g, wg, wu, wd)


def _inproj_kernel(x_ref, pre_ref, w_ref, cos_ref, sin_lo_ref, sin_hi_ref,
                   a0_ref, a1_ref, a2_ref, b_ref, slab_ref):
    h = _rms(x_ref[0], pre_ref[...]).astype(BF16)
    cos = cos_ref[...]
    sin_lo = sin_lo_ref[...]
    sin_hi = sin_hi_ref[...]
    n_slabs = A_WIDTH // LANES
    n_rot_slabs = 2 * A_GROUP_WIDTH // LANES
    for g, (d, out_ref) in enumerate(zip(DILATIONS, (a0_ref, a1_ref, a2_ref))):
        res = jnp.dot(h, w_ref[:, g * A_WIDTH:(g + 1) * A_WIDTH], preferred_element_type=F32)
        n = TOKEN_TILE // d
        for c in range(n_slabs):
            xs = res[:, c * LANES:(c + 1) * LANES]
            if c < n_rot_slabs:
                xs = (xs * cos + pltpu.roll(xs, ROT_DIM // 2, 1) * sin_hi
                      + pltpu.roll(xs, LANES - ROT_DIM // 2, 1) * sin_lo)
            if d == 1:
                out_ref[0, 0, :, c * LANES:(c + 1) * LANES] = xs.astype(BF16)
            else:
                slab_ref[c] = xs
        if d > 1:
            for c in range(n_slabs):
                for r in range(d):
                    rows = slab_ref[c, pl.ds(r, n, stride=d), :]
                    out_ref[0, r, :, c * LANES:(c + 1) * LANES] = rows.astype(BF16)
    res = jnp.dot(h, w_ref[:, 3 * A_WIDTH:], preferred_element_type=F32)
    b_ref[0] = res.astype(BF16)


def _inproj(x, pre_g, w_proj, cos_t, sin_lo_t, sin_hi_t):
    bsz, seq, _ = x.shape
    nblk = seq // TOKEN_TILE
    tab_spec = pl.BlockSpec((TOKEN_TILE, LANES), lambda b, i: (i, 0))
    out_shapes = [jax.ShapeDtypeStruct((bsz, d, seq // d, A_WIDTH), BF16) for d in DILATIONS]
    out_specs = [pl.BlockSpec((1, d, TOKEN_TILE // d, A_WIDTH), lambda b, i: (b, 0, i, 0)) for d in DILATIONS]
    out_shapes.append(jax.ShapeDtypeStruct((bsz, seq, 3 * B_WIDTH), BF16))
    out_specs.append(pl.BlockSpec((1, TOKEN_TILE, 3 * B_WIDTH), lambda b, i: (b, i, 0)))
    return pl.pallas_call(
        _inproj_kernel,
        grid=(bsz, nblk),
        in_specs=[
            pl.BlockSpec((1, TOKEN_TILE, D_MODEL), lambda b, i: (b, i, 0)),
            _const_spec((1, D_MODEL)),
            _const_spec((D_MODEL, 3 * A_WIDTH + 3 * B_WIDTH)),
            tab_spec, tab_spec, tab_spec,
        ],
        out_specs=out_specs,
        out_shape=out_shapes,
        scratch_shapes=[pltpu.VMEM((A_WIDTH // LANES, TOKEN_TILE, LANES), F32)],
        compiler_params=_params(),
    )(x, pre_g, w_proj, cos_t, sin_lo_t, sin_hi_t)


def _head_select(stacked, rows, n_heads, width):
    lane_head = lax.broadcasted_iota(jnp.int32, (rows, width), 1) // HEAD_DIM
    out = stacked[0:rows]
    for hh in range(1, n_heads):
        out = jnp.where(lane_head == hh, stacked[hh * rows:(hh + 1) * rows], out)
    return out


def _head_stack(q, n_heads):
    rows, width = q.shape
    lane_head = lax.broadcasted_iota(jnp.int32, (rows, width), 1) // HEAD_DIM
    zero = jnp.zeros_like(q)
    return jnp.concatenate([jnp.where(lane_head == hh, q, zero) for hh in range(n_heads)], axis=0)


def _softmax_rows(s):
    m = jnp.max(s, axis=-1, keepdims=True)
    p = jnp.exp(s - m)
    l = jnp.sum(p, axis=-1, keepdims=True)
    return (p / l).astype(BF16), m + jnp.log(l)


def _band_bias():
    r = BAND_RADIUS
    q_off = np.arange(BAND_Q_TILE)[:, None]
    col = np.arange(BAND_Q_TILE + 2 * r)[None, :]
    band = (col >= q_off) & (col <= q_off + 2 * r)
    variants = []
    for v in range(4):
        ok = band
        if v & 1:
            ok = ok & (col >= r)
        if v & 2:
            ok = ok & (col < BAND_Q_TILE + r)
        variants.append(np.where(ok, 0.0, NEG).astype(np.float32))
    return jnp.asarray(np.stack(variants))


def _attn_a_kernel(q_ref, kp_ref, kc_ref, kn_ref, vp_ref, vc_ref, vn_ref, bias_ref, o_ref, lse_ref, k_ext, v_ext,
                   *, q_block):
    i = pl.program_id(1)
    last = pl.num_programs(1) - 1
    r = BAND_RADIUS
    k_ext[0:r] = kp_ref[0]
    k_ext[r:r + q_block] = kc_ref[0]
    k_ext[r + q_block:] = kn_ref[0]
    v_ext[0:r] = vp_ref[0]
    v_ext[r:r + q_block] = vc_ref[0]
    v_ext[r + q_block:] = vn_ref[0]
    n_keys = BAND_Q_TILE + 2 * r
    n_tiles = q_block // BAND_Q_TILE
    lane_head = lax.broadcasted_iota(jnp.int32, (BAND_Q_TILE, A_GROUP_WIDTH), 1) // HEAD_DIM

    def scores(j):
        q = _head_stack(q_ref[0, j * BAND_Q_TILE:(j + 1) * BAND_Q_TILE, :], A_GROUP_HEADS)
        k = k_ext[j * BAND_Q_TILE:j * BAND_Q_TILE + n_keys, :]
        return lax.dot_general(q, k, (((1,), (1,)), ((), ())), preferred_element_type=F32)

    s_next = scores(0)
    for j in range(n_tiles):
        variant = 0
        if j == 0:
            variant = variant + jnp.where(i == 0, 1, 0)
        if j == n_tiles - 1:
            variant = variant + jnp.where(i == last, 2, 0)
        bias = bias_ref[variant]
        v = v_ext[j * BAND_Q_TILE:j * BAND_Q_TILE + n_keys, :]
        s = s_next
        if j + 1 < n_tiles:
            s_next = scores(j + 1)
        probs, lse = [], None
        for hh in range(A_GROUP_HEADS):
            pn, lse_h = _softmax_rows(s[hh * BAND_Q_TILE:(hh + 1) * BAND_Q_TILE] + bias)
            probs.append(pn)
            lse_h = jnp.broadcast_to(lse_h, (BAND_Q_TILE, A_GROUP_WIDTH))
            lse = lse_h if lse is None else jnp.where(lane_head == hh, lse_h, lse)
        o = jnp.dot(jnp.concatenate(probs, axis=0), v, preferred_element_type=F32)
        sl = slice(j * BAND_Q_TILE, (j + 1) * BAND_Q_TILE)
        o_ref[0, sl, :] = _head_select(o, BAND_Q_TILE, A_GROUP_HEADS, A_GROUP_WIDTH)
        lse_ref[0, sl, :] = lse


def _attn_a(qkv, band_bias):
    n_seq, seq_len, _ = qkv.shape
    q_block = min(BAND_Q_BLOCK, seq_len)
    halo_per_block = q_block // BAND_RADIUS
    n_halo = seq_len // BAND_RADIUS

    def cur(col):
        return pl.BlockSpec((1, q_block, A_GROUP_WIDTH), lambda s, i: (s, i, col))

    def prev(col):
        return pl.BlockSpec((1, BAND_RADIUS, A_GROUP_WIDTH),
                            lambda s, i: (s, jnp.maximum(i * halo_per_block - 1, 0), col))

    def nxt(col):
        return pl.BlockSpec((1, BAND_RADIUS, A_GROUP_WIDTH),
                            lambda s, i: (s, jnp.minimum((i + 1) * halo_per_block, n_halo - 1), col))

    out_spec = pl.BlockSpec((1, q_block, A_GROUP_WIDTH), lambda s, i: (s, i, 0))
    out_shape = jax.ShapeDtypeStruct((n_seq, seq_len, A_GROUP_WIDTH), F32)
    ext = pltpu.VMEM((q_block + 2 * BAND_RADIUS, A_GROUP_WIDTH), BF16)
    return pl.pallas_call(
        functools.partial(_attn_a_kernel, q_block=q_block),
        grid=(n_seq, seq_len // q_block),
        in_specs=[cur(0), prev(1), cur(1), nxt(1), prev(2), cur(2), nxt(2), _const_spec(band_bias.shape)],
        out_specs=[out_spec, out_spec],
        out_shape=[out_shape, out_shape],
        scratch_shapes=[ext, ext],
        compiler_params=_params(),
    )(qkv, qkv, qkv, qkv, qkv, qkv, qkv, band_bias)


def _attn_b_kernel(q_ref, kp_ref, kc_ref, kn_ref, vp_ref, vc_ref, vn_ref, bias_ref, o_ref, k_ext, v_ext):
    i = pl.program_id(1)
    last = pl.num_programs(1) - 1
    halo = (NA_ROWS // 2) * GRID_W
    step_tokens = B_ROWS_PER_STEP * GRID_W
    win = NA_ROWS * GRID_W
    k_ext[0:halo] = kp_ref[0]
    k_ext[halo:halo + step_tokens] = kc_ref[0]
    k_ext[halo + step_tokens:] = kn_ref[0]
    v_ext[0:halo] = vp_ref[0]
    v_ext[halo:halo + step_tokens] = vc_ref[0]
    v_ext[halo + step_tokens:] = vn_ref[0]
    lo = jnp.where(i == 0, NA_ROWS // 2, 0)
    hi = jnp.where(i == last, NA_ROWS // 2, B_ROWS_PER_STEP - 1)
    n_halves = B_HEADS // B_HALF_HEADS
    units = [(rr, half) for rr in range(B_ROWS_PER_STEP) for half in range(n_halves)]

    def window(rr):
        w_row = jnp.clip(rr, lo, hi)
        start = pl.multiple_of(w_row * GRID_W, GRID_W)
        delta = w_row - rr + (NA_ROWS // 2 - 1)
        return start, delta

    def scores(rr, half):
        start, _ = window(rr)
        lanes = slice(half * B_HALF_WIDTH, (half + 1) * B_HALF_WIDTH)
        q = _head_stack(q_ref[0, rr * GRID_W:(rr + 1) * GRID_W, lanes], B_HALF_HEADS)
        k = k_ext[pl.ds(start, win), lanes]
        return lax.dot_general(q, k, (((1,), (1,)), ((), ())), preferred_element_type=F32)

    s_next = scores(*units[0])
    for u, (rr, half) in enumerate(units):
        start, delta = window(rr)
        rows = slice(rr * GRID_W, (rr + 1) * GRID_W)
        lanes = slice(half * B_HALF_WIDTH, (half + 1) * B_HALF_WIDTH)
        s = s_next
        if u + 1 < len(units):
            s_next = scores(*units[u + 1])
        probs = []
        for hh in range(B_HALF_HEADS):
            b_rows = pl.ds((half * B_HALF_HEADS + hh) * GRID_W, GRID_W)
            pn, _ = _softmax_rows(s[hh * GRID_W:(hh + 1) * GRID_W] + bias_ref[delta, b_rows, :])
            probs.append(pn)
        v = v_ext[pl.ds(start, win), lanes]
        o = jnp.dot(jnp.concatenate(probs, axis=0), v, preferred_element_type=F32)
        o_ref[0, rows, lanes] = _head_select(o, GRID_W, B_HALF_HEADS, B_HALF_WIDTH).astype(BF16)


def _attn_b(qkv, bias):
    bsz, seq, _ = qkv.shape
    step_tokens = B_ROWS_PER_STEP * GRID_W
    halo = (NA_ROWS // 2) * GRID_W
    per = step_tokens // halo
    n_halo = seq // halo

    def cur(col):
        return pl.BlockSpec((1, step_tokens, B_WIDTH), lambda b, i: (b, i, col))

    def prev(col):
        return pl.BlockSpec((1, halo, B_WIDTH), lambda b, i: (b, jnp.maximum(i * per - 1, 0), col))

    def nxt(col):
        return pl.BlockSpec((1, halo, B_WIDTH), lambda b, i: (b, jnp.minimum((i + 1) * per, n_halo - 1), col))

    ext = pltpu.VMEM((step_tokens + 2 * halo, B_WIDTH), BF16)
    return pl.pallas_call(
        _attn_b_kernel,
        grid=(bsz, seq // step_tokens),
        in_specs=[cur(0), prev(1), cur(1), nxt(1), prev(2), cur(2), nxt(2),
                  _const_spec(bias.shape)],
        out_specs=pl.BlockSpec((1, step_tokens, B_WIDTH), lambda b, i: (b, i, 0)),
        out_shape=jax.ShapeDtypeStruct((bsz, seq, B_WIDTH), BF16),
        scratch_shapes=[ext, ext],
        compiler_params=_params(),
    )(qkv, qkv, qkv, qkv, qkv, qkv, qkv, bias)


MIX_SUB = 256


def _mixout_kernel(x_ref, o0_ref, l0_ref, o1_ref, l1_ref, o2_ref, l2_ref, ob_ref,
                   pre_ref, post_ref, wgate_ref, bgate_ref, wa_ref, wb_ref, wout_ref,
                   y_ref, nat_ref):
    n_slabs = A_GROUP_WIDTH // LANES
    for k, (d, src) in enumerate(((DILATIONS[1], o1_ref), (DILATIONS[1], l1_ref),
                                  (DILATIONS[2], o2_ref), (DILATIONS[2], l2_ref))):
        n = TOKEN_TILE // d
        for c in range(n_slabs):
            for r in range(d):
                nat_ref[k * n_slabs + c, pl.ds(r, n, stride=d), :] = src[0, r, :, c * LANES:(c + 1) * LANES]

    def nat(k, sl):
        return jnp.concatenate([nat_ref[k * n_slabs + c, sl, :] for c in range(n_slabs)], axis=-1)

    for t in range(TOKEN_TILE // MIX_SUB):
        sl = slice(t * MIX_SUB, (t + 1) * MIX_SUB)
        x = x_ref[0, sl, :]
        h = _rms(x, pre_ref[...]).astype(BF16)
        gates = jax.nn.sigmoid(jnp.dot(h, wgate_ref[...], preferred_element_type=F32) + bgate_ref[...])
        o0, l0 = o0_ref[0, 0, sl, :], l0_ref[0, 0, sl, :]
        o1, l1, o2, l2 = nat(0, sl), nat(1, sl), nat(2, sl), nat(3, sl)
        mx = jnp.maximum(jnp.maximum(l0, l1), l2)
        e0, e1, e2 = jnp.exp(l0 - mx), jnp.exp(l1 - mx), jnp.exp(l2 - mx)
        oa = (e0 * o0 + e1 * o1 + e2 * o2) / (e0 + e1 + e2)
        ya = jnp.dot(oa.astype(BF16), wa_ref[...], preferred_element_type=F32)
        yb = jnp.dot(ob_ref[0, sl, :], wb_ref[...], preferred_element_type=F32)
        mixed = (gates[:, :D_MODEL] * ya + gates[:, D_MODEL:] * yb).astype(BF16)
        m = jnp.dot(mixed, wout_ref[...], preferred_element_type=F32)
        y_ref[0, sl, :] = x + _rms(m, post_ref[...])


def _mixout(x, oas, lses, ob, pre_g, post_g, w_gate, b_gate, w_a, w_b, w_out):
    bsz, seq, _ = x.shape
    nblk = seq // TOKEN_TILE
    tile_spec = pl.BlockSpec((1, TOKEN_TILE, D_MODEL), lambda b, i: (b, i, 0))
    in_specs = [tile_spec]
    operands = [x]
    for d, o, l in zip(DILATIONS, oas, lses):
        spec = pl.BlockSpec((1, d, TOKEN_TILE // d, A_GROUP_WIDTH), lambda b, i: (b, 0, i, 0))
        in_specs += [spec, spec]
        operands += [o, l]
    in_specs.append(pl.BlockSpec((1, TOKEN_TILE, B_WIDTH), lambda b, i: (b, i, 0)))
    operands.append(ob)
    for w in (pre_g, post_g, w_gate, b_gate, w_a, w_b, w_out):
        in_specs.append(_const_spec(w.shape))
        operands.append(w)
    return pl.pallas_call(
        _mixout_kernel,
        grid=(bsz, nblk),
        in_specs=in_specs,
        out_specs=tile_spec,
        out_shape=jax.ShapeDtypeStruct((bsz, seq, D_MODEL), F32),
        scratch_shapes=[pltpu.VMEM((4 * A_GROUP_WIDTH // LANES, TOKEN_TILE, LANES), F32)],
        compiler_params=_params(),
    )(*operands)


def _rotary_tables(seq):
    half = ROT_DIM // 2
    inv_freq = ROPE_THETA ** (-jnp.arange(0, ROT_DIM, 2, dtype=F32) / ROT_DIM)
    ang = jnp.arange(seq, dtype=F32)[:, None] * inv_freq[None, :]
    cos, sin = jnp.cos(ang), jnp.sin(ang)
    ones = jnp.ones((seq, HEAD_DIM - ROT_DIM), F32)
    zeros = jnp.zeros((seq, HEAD_DIM - ROT_DIM), F32)
    zh = jnp.zeros((seq, half), F32)
    cos_h = jnp.concatenate([cos, cos, ones], axis=1)
    sin_hi_h = jnp.concatenate([zh, sin, zeros], axis=1)
    sin_lo_h = jnp.concatenate([-sin, zh, zeros], axis=1)
    reps = LANES // HEAD_DIM
    return tuple(jnp.tile(t, (1, reps)) for t in (cos_h, sin_lo_h, sin_hi_h))


def _neighbourhood_bias(rpb):
    cj = np.arange(GRID_W)
    cs = np.clip(cj - NA_COLS // 2, 0, GRID_W - NA_COLS)
    col_valid = (cj[None, :] >= cs[:, None]) & (cj[None, :] < cs[:, None] + NA_COLS)
    dc_idx = np.clip(cj[None, :] - cj[:, None] + (NA_COLS - 1), 0, 2 * NA_COLS - 2)
    n_dc = 2 * NA_COLS - 1
    onehot = (dc_idx[None] == np.arange(n_dc)[:, None, None]).astype(np.float32)
    base = jnp.einsum("hrd,dqk->hrqk", rpb.astype(F32), jnp.asarray(onehot), precision=lax.Precision.HIGHEST)
    base = jnp.where(col_valid[None, None], base, NEG)
    tables = []
    for first in range(NA_ROWS):
        t = base[:, first:first + NA_ROWS].transpose(0, 2, 1, 3)
        tables.append(t.reshape(B_HEADS * GRID_W, NA_ROWS * GRID_W))
    return jnp.stack(tables, axis=0)


def _split_w_in(w_in):
    scale = HEAD_DIM ** -0.5
    o = 0
    qa = w_in[:, o:o + A_WIDTH] * scale; o += A_WIDTH
    ka = w_in[:, o:o + A_WIDTH]; o += A_WIDTH
    va = w_in[:, o:o + A_WIDTH]; o += A_WIDTH
    qb = w_in[:, o:o + B_WIDTH] * scale; o += B_WIDTH
    kb = w_in[:, o:o + B_WIDTH]; o += B_WIDTH
    vb = w_in[:, o:o + B_WIDTH]; o += B_WIDTH
    w_gate = w_in[:, o:]
    cols = []
    for g in range(len(DILATIONS)):
        sl = slice(g * A_GROUP_WIDTH, (g + 1) * A_GROUP_WIDTH)
        cols += [qa[:, sl], ka[:, sl], va[:, sl]]
    cols += [qb, kb, vb]
    return jnp.concatenate(cols, axis=1).astype(BF16), w_gate.astype(BF16)


def _encoder(x, p, tables):
    bsz, seq, _ = x.shape
    row = lambda v: v.reshape(1, -1)
    x1 = _ffn(x.reshape(bsz * seq, D_MODEL), row(p["ffn1_pre_g"]), row(p["ffn1_post_g"]),
              p["ffn1_wg"], p["ffn1_wu"], p["ffn1_wd"]).reshape(bsz, seq, D_MODEL)
    cos_t, sin_lo_t, sin_hi_t = tables
    a0, a1, a2, qkvb = _inproj(x1, row(p["mix_pre_g"]), p["w_proj"], cos_t, sin_lo_t, sin_hi_t)
    oas, lses = [], []
    band_bias = _band_bias()
    for d, qkv in zip(DILATIONS, (a0, a1, a2)):
        o, lse = _attn_a(qkv.reshape(bsz * d, seq // d, A_WIDTH), band_bias)
        oas.append(o.reshape(bsz, d, seq // d, A_GROUP_WIDTH))
        lses.append(lse.reshape(bsz, d, seq // d, A_GROUP_WIDTH))
    ob = _attn_b(qkvb, p["bias_b"])
    x2 = _mixout(x1, oas, lses, ob, row(p["mix_pre_g"]), row(p["mix_post_g"]), p["w_gate"], row(p["b_gate"]),
                 p["w_a"], p["w_b"], p["w_out"])
    x3 = _ffn(x2.reshape(bsz * seq, D_MODEL), row(p["ffn2_pre_g"]), row(p["ffn2_post_g"]),
              p["ffn2_wg"], p["ffn2_wu"], p["ffn2_wd"])
    return x3.reshape(bsz, seq, D_MODEL)


def kernel(x_prompt, x_sample, ffn1_pre_g, ffn1_post_g, ffn1_w_gate, ffn1_w_up, ffn1_w_down, mix_pre_g, mix_post_g,
           w_in, b_gate, rpb, w_branch_a, w_branch_b, w_out, ffn2_pre_g, ffn2_post_g, ffn2_w_gate, ffn2_w_up,
           ffn2_w_down):
    w_proj, w_gate = _split_w_in(w_in[0])
    p = {
        "ffn1_pre_g": ffn1_pre_g[0], "ffn1_post_g": ffn1_post_g[0],
        "ffn1_wg": ffn1_w_gate[0].astype(BF16), "ffn1_wu": ffn1_w_up[0].astype(BF16),
        "ffn1_wd": ffn1_w_down[0].astype(BF16),
        "mix_pre_g": mix_pre_g[0], "mix_post_g": mix_post_g[0],
        "w_proj": w_proj, "w_gate": w_gate, "b_gate": b_gate[0],
        "bias_b": _neighbourhood_bias(rpb[0]),
        "w_a": w_branch_a[0].astype(BF16), "w_b": w_branch_b[0].astype(BF16), "w_out": w_out[0].astype(BF16),
        "ffn2_pre_g": ffn2_pre_g[0], "ffn2_post_g": ffn2_post_g[0],
        "ffn2_wg": ffn2_w_gate[0].astype(BF16), "ffn2_wu": ffn2_w_up[0].astype(BF16),
        "ffn2_wd": ffn2_w_down[0].astype(BF16),
    }
    tables = _rotary_tables(max(x_prompt.shape[1], x_sample.shape[1]))
    return _encoder(x_prompt, p, tables), _encoder(x_sample, p, tables)
```

```python
import functools

import jax
import jax.numpy as jnp
import numpy as np
from jax import lax
from jax.experimental import pallas as pl
from jax.experimental.pallas import tpu as pltpu

D_MODEL = 1024
D_FF = 2816
HEAD_DIM = 64
DILATIONS = (1, 4, 16)
BAND_RADIUS = 64
A_GROUP_HEADS = 4
A_GROUP_WIDTH = A_GROUP_HEADS * HEAD_DIM
A_WIDTH = 3 * A_GROUP_WIDTH
B_HEADS = 8
B_WIDTH = B_HEADS * HEAD_DIM
GRID_W = 64
NA_ROWS = 8
NA_COLS = 16
ROPE_THETA = 500000.0
ROT_DIM = HEAD_DIM // 4
RMS_EPS = 1e-6
NEG = -1e30

LANES = 128
VMEM_LIMIT = 56 * 1024 * 1024

TOKEN_TILE = 1024
FFN_TILE = 1024
FFN_SUB = 256
FF_CHUNKS = ((0, 1024), (1024, 1024), (2048, 768))
BAND_Q_BLOCK = 2048
BAND_Q_TILE = 128
B_ROWS_PER_STEP = 16
B_HALF_HEADS = 4
B_HALF_WIDTH = B_HALF_HEADS * HEAD_DIM

BF16 = jnp.bfloat16
F32 = jnp.float32


def _rms(x, g):
    return x * lax.rsqrt(jnp.mean(x * x, axis=-1, keepdims=True) + RMS_EPS) * g


def _const_spec(shape):
    zeros = (0,) * len(shape)
    return pl.BlockSpec(shape, lambda *_: zeros, pipeline_mode=pl.Buffered(1))


def _params():
    return pltpu.CompilerParams(vmem_limit_bytes=VMEM_LIMIT)


def _ffn_kernel(x_ref, pre_ref, post_ref, wg_ref, wu_ref, wd_ref, o_ref):
    n_sub = FFN_TILE // FFN_SUB

    def rows(t):
        return slice(t * FFN_SUB, (t + 1) * FFN_SUB)

    def pre(t):
        return _rms(x_ref[rows(t), :], pre_ref[...]).astype(BF16)

    def finish(t, y):
        o_ref[rows(t), :] = x_ref[rows(t), :] + 0.5 * _rms(y, post_ref[...])

    h = pre(0)
    y_prev = None
    for t in range(n_sub):
        y = None
        h_next = None
        for c, (start, size) in enumerate(FF_CHUNKS):
            g = jnp.dot(h, wg_ref[:, start:start + size], preferred_element_type=F32)
            u = jnp.dot(h, wu_ref[:, start:start + size], preferred_element_type=F32)
            a = (g * jax.nn.sigmoid(g) * u).astype(BF16)
            part = jnp.dot(a, wd_ref[start:start + size, :], preferred_element_type=F32)
            y = part if y is None else y + part
            if c == 0:
                if t + 1 < n_sub:
                    h_next = pre(t + 1)
                if t > 0:
                    finish(t - 1, y_prev)
        h, y_prev = h_next, y
    finish(n_sub - 1, y_prev)


def _ffn(x2d, pre_g, post_g, wg, wu, wd):
    m = x2d.shape[0]
    return pl.pallas_call(
        _ffn_kernel,
        grid=(m // FFN_TILE,),
        in_specs=[
            pl.BlockSpec((FFN_TILE, D_MODEL), lambda i: (i, 0)),
            _const_spec((1, D_MODEL)),
            _const_spec((1, D_MODEL)),
            _const_spec((D_MODEL, D_FF)),
            _const_spec((D_MODEL, D_FF)),
            _const_spec((D_FF, D_MODEL)),
        ],
        out_specs=pl.BlockSpec((FFN_TILE, D_MODEL), lambda i: (i, 0)),
        out_shape=jax.ShapeDtypeStruct((m, D_MODEL), F32),
        compiler_params=_params(),
    )(x2d, pre_g, post_g, wg, wu, wd)


def _inproj_kernel(x_ref, pre_ref, w_ref, cos_ref, sin_lo_ref, sin_hi_ref,
                   a0_ref, a1_ref, a2_ref, b_ref, slab_ref):
    h = _rms(x_ref[0], pre_ref[...]).astype(BF16)
    cos = cos_ref[...]
    sin_lo = sin_lo_ref[...]
    sin_hi = sin_hi_ref[...]
    n_slabs = A_WIDTH // LANES
    n_rot_slabs = 2 * A_GROUP_WIDTH // LANES
    for g, (d, out_ref) in enumerate(zip(DILATIONS, (a0_ref, a1_ref, a2_ref))):
        res = jnp.dot(h, w_ref[:, g * A_WIDTH:(g + 1) * A_WIDTH], preferred_element_type=F32)
        n = TOKEN_TILE // d
        for c in range(n_slabs):
            xs = res[:, c * LANES:(c + 1) * LANES]
            if c < n_rot_slabs:
                xs = (xs * cos + pltpu.roll(xs, ROT_DIM // 2, 1) * sin_hi
                      + pltpu.roll(xs, LANES - ROT_DIM // 2, 1) * sin_lo)
            if d == 1:
                out_ref[0, 0, :, c * LANES:(c + 1) * LANES] = xs.astype(BF16)
            else:
                slab_ref[c] = xs
        if d > 1:
            for c in range(n_slabs):
                for r in range(d):
                    rows = slab_ref[c, pl.ds(r, n, stride=d), :]
                    out_ref[0, r, :, c * LANES:(c + 1) * LANES] = rows.astype(BF16)
    res = jnp.dot(h, w_ref[:, 3 * A_WIDTH:], preferred_element_type=F32)
    b_ref[0] = res.astype(BF16)


def _inproj(x, pre_g, w_proj, cos_t, sin_lo_t, sin_hi_t):
    bsz, seq, _ = x.shape
    nblk = seq // TOKEN_TILE
    tab_spec = pl.BlockSpec((TOKEN_TILE, LANES), lambda b, i: (i, 0))
    out_shapes = [jax.ShapeDtypeStruct((bsz, d, seq // d, A_WIDTH), BF16) for d in DILATIONS]
    out_specs = [pl.BlockSpec((1, d, TOKEN_TILE // d, A_WIDTH), lambda b, i: (b, 0, i, 0)) for d in DILATIONS]
    out_shapes.append(jax.ShapeDtypeStruct((bsz, seq, 3 * B_WIDTH), BF16))
    out_specs.append(pl.BlockSpec((1, TOKEN_TILE, 3 * B_WIDTH), lambda b, i: (b, i, 0)))
    return pl.pallas_call(
        _inproj_kernel,
        grid=(bsz, nblk),
        in_specs=[
            pl.BlockSpec((1, TOKEN_TILE, D_MODEL), lambda b, i: (b, i, 0)),
            _const_spec((1, D_MODEL)),
            _const_spec((D_MODEL, 3 * A_WIDTH + 3 * B_WIDTH)),
            tab_spec, tab_spec, tab_spec,
        ],
        out_specs=out_specs,
        out_shape=out_shapes,
        scratch_shapes=[pltpu.VMEM((A_WIDTH // LANES, TOKEN_TILE, LANES), F32)],
        compiler_params=_params(),
    )(x, pre_g, w_proj, cos_t, sin_lo_t, sin_hi_t)


def _head_stack(q, n_heads):
    rows, width = q.shape
    lane_head = lax.broadcasted_iota(jnp.int32, (rows, width), 1) // HEAD_DIM
    zero = jnp.zeros_like(q)
    return jnp.concatenate([jnp.where(lane_head == hh, q, zero) for hh in range(n_heads)], axis=0)


def _softmax_parts(s):
    m = jnp.max(s, axis=-1, keepdims=True)
    p = jnp.exp(s - m)
    l = jnp.sum(p, axis=-1, keepdims=True)
    return p.astype(BF16), 1.0 / l, m + jnp.log(l)


def _band_bias():
    r = BAND_RADIUS
    q_off = np.arange(BAND_Q_TILE)[:, None]
    col = np.arange(BAND_Q_TILE + 2 * r)[None, :]
    band = (col >= q_off) & (col <= q_off + 2 * r)
    variants = []
    for v in range(4):
        ok = band
        if v & 1:
            ok = ok & (col >= r)
        if v & 2:
            ok = ok & (col < BAND_Q_TILE + r)
        variants.append(np.where(ok, 0.0, NEG).astype(np.float32))
    return jnp.asarray(np.stack(variants))


def _attn_a_kernel(q_ref, kp_ref, kc_ref, kn_ref, vp_ref, vc_ref, vn_ref, bias_ref, o_ref, lse_ref, k_ext, v_ext,
                   *, q_block):
    i = pl.program_id(1)
    last = pl.num_programs(1) - 1
    r = BAND_RADIUS
    k_ext[0:r] = kp_ref[0]
    k_ext[r:r + q_block] = kc_ref[0]
    k_ext[r + q_block:] = kn_ref[0]
    v_ext[0:r] = vp_ref[0]
    v_ext[r:r + q_block] = vc_ref[0]
    v_ext[r + q_block:] = vn_ref[0]
    n_keys = BAND_Q_TILE + 2 * r
    n_tiles = q_block // BAND_Q_TILE

    def scores(j):
        q = _head_stack(q_ref[0, j * BAND_Q_TILE:(j + 1) * BAND_Q_TILE, :], A_GROUP_HEADS)
        k = k_ext[j * BAND_Q_TILE:j * BAND_Q_TILE + n_keys, :]
        return lax.dot_general(q, k, (((1,), (1,)), ((), ())), preferred_element_type=F32)

    s_next = scores(0)
    for j in range(n_tiles):
        variant = 0
        if j == 0:
            variant = variant + jnp.where(i == 0, 1, 0)
        if j == n_tiles - 1:
            variant = variant + jnp.where(i == last, 2, 0)
        v = v_ext[j * BAND_Q_TILE:j * BAND_Q_TILE + n_keys, :]
        s = s_next
        if j + 1 < n_tiles:
            s_next = scores(j + 1)
        sl = slice(j * BAND_Q_TILE, (j + 1) * BAND_Q_TILE)
        probs, scales = [], []
        for hh in range(A_GROUP_HEADS):
            p, inv_l, lse_h = _softmax_parts(s[hh * BAND_Q_TILE:(hh + 1) * BAND_Q_TILE] + bias_ref[variant])
            probs.append(p)
            scales.append(inv_l)
            lse_ref[0, sl, hh * HEAD_DIM:(hh + 1) * HEAD_DIM] = jnp.broadcast_to(lse_h, (BAND_Q_TILE, HEAD_DIM))
        o = jnp.dot(jnp.concatenate(probs, axis=0), v, preferred_element_type=F32)
        for hh in range(A_GROUP_HEADS):
            lanes = slice(hh * HEAD_DIM, (hh + 1) * HEAD_DIM)
            o_ref[0, sl, lanes] = o[hh * BAND_Q_TILE:(hh + 1) * BAND_Q_TILE, lanes] * scales[hh]


def _attn_a(qkv, band_bias):
    n_seq, seq_len, _ = qkv.shape
    q_block = min(BAND_Q_BLOCK, seq_len)
    halo_per_block = q_block // BAND_RADIUS
    n_halo = seq_len // BAND_RADIUS

    def cur(col):
        return pl.BlockSpec((1, q_block, A_GROUP_WIDTH), lambda s, i: (s, i, col))

    def prev(col):
        return pl.BlockSpec((1, BAND_RADIUS, A_GROUP_WIDTH),
                            lambda s, i: (s, jnp.maximum(i * halo_per_block - 1, 0), col))

    def nxt(col):
        return pl.BlockSpec((1, BAND_RADIUS, A_GROUP_WIDTH),
                            lambda s, i: (s, jnp.minimum((i + 1) * halo_per_block, n_halo - 1), col))

    out_spec = pl.BlockSpec((1, q_block, A_GROUP_WIDTH), lambda s, i: (s, i, 0))
    out_shape = jax.ShapeDtypeStruct((n_seq, seq_len, A_GROUP_WIDTH), F32)
    ext = pltpu.VMEM((q_block + 2 * BAND_RADIUS, A_GROUP_WIDTH), BF16)
    return pl.pallas_call(
        functools.partial(_attn_a_kernel, q_block=q_block),
        grid=(n_seq, seq_len // q_block),
        in_specs=[cur(0), prev(1), cur(1), nxt(1), prev(2), cur(2), nxt(2), _const_spec(band_bias.shape)],
        out_specs=[out_spec, out_spec],
        out_shape=[out_shape, out_shape],
        scratch_shapes=[ext, ext],
        compiler_params=_params(),
    )(qkv, qkv, qkv, qkv, qkv, qkv, qkv, band_bias)


def _attn_b_kernel(q_ref, kp_ref, kc_ref, kn_ref, vp_ref, vc_ref, vn_ref, bias_ref, o_ref, k_ext, v_ext):
    i = pl.program_id(1)
    last = pl.num_programs(1) - 1
    halo = (NA_ROWS // 2) * GRID_W
    step_tokens = B_ROWS_PER_STEP * GRID_W
    win = NA_ROWS * GRID_W
    k_ext[0:halo] = kp_ref[0]
    k_ext[halo:halo + step_tokens] = kc_ref[0]
    k_ext[halo + step_tokens:] = kn_ref[0]
    v_ext[0:halo] = vp_ref[0]
    v_ext[halo:halo + step_tokens] = vc_ref[0]
    v_ext[halo + step_tokens:] = vn_ref[0]
    lo = jnp.where(i == 0, NA_ROWS // 2, 0)
    hi = jnp.where(i == last, B_ROWS_PER_STEP - NA_ROWS // 2, B_ROWS_PER_STEP - 1)
    n_halves = B_HEADS // B_HALF_HEADS
    units = [(rr, half) for rr in range(B_ROWS_PER_STEP) for half in range(n_halves)]

    def window(rr):
        w_row = jnp.clip(rr, lo, hi)
        start = pl.multiple_of(w_row * GRID_W, GRID_W)
        delta = w_row - rr + (NA_ROWS // 2 - 1)
        return start, delta

    def scores(rr, half):
        start, _ = window(rr)
        lanes = slice(half * B_HALF_WIDTH, (half + 1) * B_HALF_WIDTH)
        q = _head_stack(q_ref[0, rr * GRID_W:(rr + 1) * GRID_W, lanes], B_HALF_HEADS)
        k = k_ext[pl.ds(start, win), lanes]
        return lax.dot_general(q, k, (((1,), (1,)), ((), ())), preferred_element_type=F32)

    s_next = scores(*units[0])
    for u, (rr, half) in enumerate(units):
        start, delta = window(rr)
        rows = slice(rr * GRID_W, (rr + 1) * GRID_W)
        lanes = slice(half * B_HALF_WIDTH, (half + 1) * B_HALF_WIDTH)
        s = s_next
        if u + 1 < len(units):
            s_next = scores(*units[u + 1])
        probs, scales = [], []
        for hh in range(B_HALF_HEADS):
            b_rows = pl.ds((half * B_HALF_HEADS + hh) * GRID_W, GRID_W)
            p, inv_l, _ = _softmax_parts(s[hh * GRID_W:(hh + 1) * GRID_W] + bias_ref[delta, b_rows, :])
            probs.append(p)
            scales.append(inv_l)
        v = v_ext[pl.ds(start, win), lanes]
        o = jnp.dot(jnp.concatenate(probs, axis=0), v, preferred_element_type=F32)
        for hh in range(B_HALF_HEADS):
            h_lanes = slice(hh * HEAD_DIM, (hh + 1) * HEAD_DIM)
            o_h = o[hh * GRID_W:(hh + 1) * GRID_W, h_lanes] * scales[hh]
            col = half * B_HALF_WIDTH + hh * HEAD_DIM
            o_ref[0, rows, col:col + HEAD_DIM] = o_h.astype(BF16)


def _attn_b(qkv, bias):
    bsz, seq, _ = qkv.shape
    step_tokens = B_ROWS_PER_STEP * GRID_W
    halo = (NA_ROWS // 2) * GRID_W
    per = step_tokens // halo
    n_halo = seq // halo

    def cur(col):
        return pl.BlockSpec((1, step_tokens, B_WIDTH), lambda b, i: (b, i, col))

    def prev(col):
        return pl.BlockSpec((1, halo, B_WIDTH), lambda b, i: (b, jnp.maximum(i * per - 1, 0), col))

    def nxt(col):
        return pl.BlockSpec((1, halo, B_WIDTH), lambda b, i: (b, jnp.minimum((i + 1) * per, n_halo - 1), col))

    ext = pltpu.VMEM((step_tokens + 2 * halo, B_WIDTH), BF16)
    return pl.pallas_call(
        _attn_b_kernel,
        grid=(bsz, seq // step_tokens),
        in_specs=[cur(0), prev(1), cur(1), nxt(1), prev(2), cur(2), nxt(2),
                  _const_spec(bias.shape)],
        out_specs=pl.BlockSpec((1, step_tokens, B_WIDTH), lambda b, i: (b, i, 0)),
        out_shape=jax.ShapeDtypeStruct((bsz, seq, B_WIDTH), BF16),
        scratch_shapes=[ext, ext],
        compiler_params=_params(),
    )(qkv, qkv, qkv, qkv, qkv, qkv, qkv, bias)


MIX_SUB = 256


def _mixout_kernel(x_ref, o0_ref, l0_ref, o1_ref, l1_ref, o2_ref, l2_ref, ob_ref,
                   pre_ref, post_ref, wgate_ref, bgate_ref, wa_ref, wb_ref, wout_ref,
                   y_ref, nat_ref):
    n_slabs = A_GROUP_WIDTH // LANES
    for k, (d, src) in enumerate(((DILATIONS[1], o1_ref), (DILATIONS[1], l1_ref),
                                  (DILATIONS[2], o2_ref), (DILATIONS[2], l2_ref))):
        n = TOKEN_TILE // d
        for c in range(n_slabs):
            for r in range(d):
                nat_ref[k * n_slabs + c, pl.ds(r, n, stride=d), :] = src[0, r, :, c * LANES:(c + 1) * LANES]

    def nat(k, sl):
        return jnp.concatenate([nat_ref[k * n_slabs + c, sl, :] for c in range(n_slabs)], axis=-1)

    for t in range(TOKEN_TILE // MIX_SUB):
        sl = slice(t * MIX_SUB, (t + 1) * MIX_SUB)
        x = x_ref[0, sl, :]
        h = _rms(x, pre_ref[...]).astype(BF16)
        gates = jax.nn.sigmoid(jnp.dot(h, wgate_ref[...], preferred_element_type=F32) + bgate_ref[...])
        o0, l0 = o0_ref[0, 0, sl, :], l0_ref[0, 0, sl, :]
        o1, l1, o2, l2 = nat(0, sl), nat(1, sl), nat(2, sl), nat(3, sl)
        mx = jnp.maximum(jnp.maximum(l0, l1), l2)
        e0, e1, e2 = jnp.exp(l0 - mx), jnp.exp(l1 - mx), jnp.exp(l2 - mx)
        oa = (e0 * o0 + e1 * o1 + e2 * o2) / (e0 + e1 + e2)
        ya = jnp.dot(oa.astype(BF16), wa_ref[...], preferred_element_type=F32)
        yb = jnp.dot(ob_ref[0, sl, :], wb_ref[...], preferred_element_type=F32)
        mixed = (gates[:, :D_MODEL] * ya + gates[:, D_MODEL:] * yb).astype(BF16)
        m = jnp.dot(mixed, wout_ref[...], preferred_element_type=F32)
        y_ref[0, sl, :] = x + _rms(m, post_ref[...])


def _mixout(x, oas, lses, ob, pre_g, post_g, w_gate, b_gate, w_a, w_b, w_out):
    bsz, seq, _ = x.shape
    nblk = seq // TOKEN_TILE
    tile_spec = pl.BlockSpec((1, TOKEN_TILE, D_MODEL), lambda b, i: (b, i, 0))
    in_specs = [tile_spec]
    operands = [x]
    for d, o, l in zip(DILATIONS, oas, lses):
        spec = pl.BlockSpec((1, d, TOKEN_TILE // d, A_GROUP_WIDTH), lambda b, i: (b, 0, i, 0))
        in_specs += [spec, spec]
        operands += [o, l]
    in_specs.append(pl.BlockSpec((1, TOKEN_TILE, B_WIDTH), lambda b, i: (b, i, 0)))
    operands.append(ob)
    for w in (pre_g, post_g, w_gate, b_gate, w_a, w_b, w_out):
        in_specs.append(_const_spec(w.shape))
        operands.append(w)
    return pl.pallas_call(
        _mixout_kernel,
        grid=(bsz, nblk),
        in_specs=in_specs,
        out_specs=tile_spec,
        out_shape=jax.ShapeDtypeStruct((bsz, seq, D_MODEL), F32),
        scratch_shapes=[pltpu.VMEM((4 * A_GROUP_WIDTH // LANES, TOKEN_TILE, LANES), F32)],
        compiler_params=_params(),
    )(*operands)


def _rotary_tables(seq):
    half = ROT_DIM // 2
    inv_freq = ROPE_THETA ** (-jnp.arange(0, ROT_DIM, 2, dtype=F32) / ROT_DIM)
    ang = jnp.arange(seq, dtype=F32)[:, None] * inv_freq[None, :]
    cos, sin = jnp.cos(ang), jnp.sin(ang)
    ones = jnp.ones((seq, HEAD_DIM - ROT_DIM), F32)
    zeros = jnp.zeros((seq, HEAD_DIM - ROT_DIM), F32)
    zh = jnp.zeros((seq, half), F32)
    cos_h = jnp.concatenate([cos, cos, ones], axis=1)
    sin_hi_h = jnp.concatenate([zh, sin, zeros], axis=1)
    sin_lo_h = jnp.concatenate([-sin, zh, zeros], axis=1)
    reps = LANES // HEAD_DIM
    return tuple(jnp.tile(t, (1, reps)) for t in (cos_h, sin_lo_h, sin_hi_h))


def _neighbourhood_bias(rpb):
    cj = np.arange(GRID_W)
    cs = np.clip(cj - NA_COLS // 2, 0, GRID_W - NA_COLS)
    col_valid = (cj[None, :] >= cs[:, None]) & (cj[None, :] < cs[:, None] + NA_COLS)
    dc_idx = np.clip(cj[None, :] - cj[:, None] + (NA_COLS - 1), 0, 2 * NA_COLS - 2)
    n_dc = 2 * NA_COLS - 1
    onehot = (dc_idx[None] == np.arange(n_dc)[:, None, None]).astype(np.float32)
    base = jnp.einsum("hrd,dqk->hrqk", rpb.astype(F32), jnp.asarray(onehot), precision=lax.Precision.HIGHEST)
    base = jnp.where(col_valid[None, None], base, NEG)
    tables = []
    for first in range(NA_ROWS):
        t = base[:, first:first + NA_ROWS].transpose(0, 2, 1, 3)
        tables.append(t.reshape(B_HEADS * GRID_W, NA_ROWS * GRID_W))
    return jnp.stack(tables, axis=0)


def _split_w_in(w_in):
    scale = HEAD_DIM ** -0.5
    o = 0
    qa = w_in[:, o:o + A_WIDTH] * scale; o += A_WIDTH
    ka = w_in[:, o:o + A_WIDTH]; o += A_WIDTH
    va = w_in[:, o:o + A_WIDTH]; o += A_WIDTH
    qb = w_in[:, o:o + B_WIDTH] * scale; o += B_WIDTH
    kb = w_in[:, o:o + B_WIDTH]; o += B_WIDTH
    vb = w_in[:, o:o + B_WIDTH]; o += B_WIDTH
    w_gate = w_in[:, o:]
    cols = []
    for g in range(len(DILATIONS)):
        sl = slice(g * A_GROUP_WIDTH, (g + 1) * A_GROUP_WIDTH)
        cols += [qa[:, sl], ka[:, sl], va[:, sl]]
    cols += [qb, kb, vb]
    return jnp.concatenate(cols, axis=1).astype(BF16), w_gate.astype(BF16)


def _encoder(x, p, tables):
    bsz, seq, _ = x.shape
    row = lambda v: v.reshape(1, -1)
    x1 = _ffn(x.reshape(bsz * seq, D_MODEL), row(p["ffn1_pre_g"]), row(p["ffn1_post_g"]),
              p["ffn1_wg"], p["ffn1_wu"], p["ffn1_wd"]).reshape(bsz, seq, D_MODEL)
    cos_t, sin_lo_t, sin_hi_t = tables
    a0, a1, a2, qkvb = _inproj(x1, row(p["mix_pre_g"]), p["w_proj"], cos_t, sin_lo_t, sin_hi_t)
    oas, lses = [], []
    band_bias = _band_bias()
    for d, qkv in zip(DILATIONS, (a0, a1, a2)):
        o, lse = _attn_a(qkv.reshape(bsz * d, seq // d, A_WIDTH), band_bias)
        oas.append(o.reshape(bsz, d, seq // d, A_GROUP_WIDTH))
        lses.append(lse.reshape(bsz, d, seq // d, A_GROUP_WIDTH))
    ob = _attn_b(qkvb, p["bias_b"])
    x2 = _mixout(x1, oas, lses, ob, row(p["mix_pre_g"]), row(p["mix_post_g"]), p["w_gate"], row(p["b_gate"]),
                 p["w_a"], p["w_b"], p["w_out"])
    x3 = _ffn(x2.reshape(bsz * seq, D_MODEL), row(p["ffn2_pre_g"]), row(p["ffn2_post_g"]),
              p["ffn2_wg"], p["ffn2_wu"], p["ffn2_wd"])
    return x3.reshape(bsz, seq, D_MODEL)


def kernel(x_prompt, x_sample, ffn1_pre_g, ffn1_post_g, ffn1_w_gate, ffn1_w_up, ffn1_w_down, mix_pre_g, mix_post_g,
           w_in, b_gate, rpb, w_branch_a, w_branch_b, w_out, ffn2_pre_g, ffn2_post_g, ffn2_w_gate, ffn2_w_up,
           ffn2_w_down):
    w_proj, w_gate = _split_w_in(w_in[0])
    p = {
        "ffn1_pre_g": ffn1_pre_g[0], "ffn1_post_g": ffn1_post_g[0],
        "ffn1_wg": ffn1_w_gate[0].astype(BF16), "ffn1_wu": ffn1_w_up[0].astype(BF16),
        "ffn1_wd": ffn1_w_down[0].astype(BF16),
        "mix_pre_g": mix_pre_g[0], "mix_post_g": mix_post_g[0],
        "w_proj": w_proj, "w_gate": w_gate, "b_gate": b_gate[0],
        "bias_b": _neighbourhood_bias(rpb[0]),
        "w_a": w_branch_a[0].astype(BF16), "w_b": w_branch_b[0].astype(BF16), "w_out": w_out[0].astype(BF16),
        "ffn2_pre_g": ffn2_pre_g[0], "ffn2_post_g": ffn2_post_g[0],
        "ffn2_wg": ffn2_w_gate[0].astype(BF16), "ffn2_wu": ffn2_w_up[0].astype(BF16),
        "ffn2_wd": ffn2_w_down[0].astype(BF16),
    }
    tables = _rotary_tables(max(x_prompt.shape[1], x_sample.shape[1]))
    return _encoder(x_prompt, p, tables), _encoder(x_sample, p, tables)
```

```python
import functools

import jax
import jax.numpy as jnp
import numpy as np
from jax import lax
from jax.experimental import pallas as pl
from jax.experimental.pallas import tpu as pltpu

D_MODEL = 1024
D_FF = 2816
HEAD_DIM = 64
DILATIONS = (1, 4, 16)
BAND_RADIUS = 64
A_GROUP_HEADS = 4
A_GROUP_WIDTH = A_GROUP_HEADS * HEAD_DIM
A_WIDTH = 3 * A_GROUP_WIDTH
B_HEADS = 8
B_WIDTH = B_HEADS * HEAD_DIM
GRID_W = 64
NA_ROWS = 8
NA_COLS = 16
ROPE_THETA = 500000.0
ROT_DIM = HEAD_DIM // 4
RMS_EPS = 1e-6
NEG = -1e30

LANES = 128
VMEM_LIMIT = 56 * 1024 * 1024

TOKEN_TILE = 1024
DEINTERLEAVE_STRIDE = 4
FFN_TILE = 1024
FFN_SUB = 256
FF_CHUNKS = ((0, 1024), (1024, 1024), (2048, 768))
BAND_Q_BLOCK = 2048
BAND_Q_TILE = 128
B_ROWS_PER_STEP = 16
B_HALF_HEADS = 4
B_HALF_WIDTH = B_HALF_HEADS * HEAD_DIM

BF16 = jnp.bfloat16
F32 = jnp.float32


def _rms(x, g):
    return x * lax.rsqrt(jnp.mean(x * x, axis=-1, keepdims=True) + RMS_EPS) * g


def _const_spec(shape):
    zeros = (0,) * len(shape)
    return pl.BlockSpec(shape, lambda *_: zeros, pipeline_mode=pl.Buffered(1))


def _params():
    return pltpu.CompilerParams(vmem_limit_bytes=VMEM_LIMIT)


def _ffn_kernel(x_ref, pre_ref, post_ref, wg_ref, wu_ref, wd_ref, o_ref):
    n_sub = FFN_TILE // FFN_SUB

    def rows(t):
        return slice(t * FFN_SUB, (t + 1) * FFN_SUB)

    def pre(t):
        return _rms(x_ref[rows(t), :], pre_ref[...]).astype(BF16)

    def finish(t, y):
        o_ref[rows(t), :] = x_ref[rows(t), :] + 0.5 * _rms(y, post_ref[...])

    h = pre(0)
    y_prev = None
    for t in range(n_sub):
        y = None
        h_next = None
        for c, (start, size) in enumerate(FF_CHUNKS):
            g = jnp.dot(h, wg_ref[:, start:start + size], preferred_element_type=F32)
            u = jnp.dot(h, wu_ref[:, start:start + size], preferred_element_type=F32)
            a = (g * jax.nn.sigmoid(g) * u).astype(BF16)
            part = jnp.dot(a, wd_ref[start:start + size, :], preferred_element_type=F32)
            y = part if y is None else y + part
            if c == 0:
                if t + 1 < n_sub:
                    h_next = pre(t + 1)
                if t > 0:
                    finish(t - 1, y_prev)
        h, y_prev = h_next, y
    finish(n_sub - 1, y_prev)


def _ffn(x2d, pre_g, post_g, wg, wu, wd):
    m = x2d.shape[0]
    return pl.pallas_call(
        _ffn_kernel,
        grid=(m // FFN_TILE,),
        in_specs=[
            pl.BlockSpec((FFN_TILE, D_MODEL), lambda i: (i, 0)),
            _const_spec((1, D_MODEL)),
            _const_spec((1, D_MODEL)),
            _const_spec((D_MODEL, D_FF)),
            _const_spec((D_MODEL, D_FF)),
            _const_spec((D_FF, D_MODEL)),
        ],
        out_specs=pl.BlockSpec((FFN_TILE, D_MODEL), lambda i: (i, 0)),
        out_shape=jax.ShapeDtypeStruct((m, D_MODEL), F32),
        compiler_params=_params(),
    )(x2d, pre_g, post_g, wg, wu, wd)


def _inproj_kernel(x_ref, pre_ref, w_ref, cos_ref, sin_lo_ref, sin_hi_ref,
                   a0_ref, a1_ref, a2_ref, b_ref, slab_ref, tmp_ref):
    h = _rms(x_ref[0], pre_ref[...]).astype(BF16)
    cos = cos_ref[...]
    sin_lo = sin_lo_ref[...]
    sin_hi = sin_hi_ref[...]
    n_slabs = A_WIDTH // LANES
    n_rot_slabs = 2 * A_GROUP_WIDTH // LANES
    for g, (d, out_ref) in enumerate(zip(DILATIONS, (a0_ref, a1_ref, a2_ref))):
        res = jnp.dot(h, w_ref[:, g * A_WIDTH:(g + 1) * A_WIDTH], preferred_element_type=F32)
        n = TOKEN_TILE // d
        for c in range(n_slabs):
            xs = res[:, c * LANES:(c + 1) * LANES]
            if c < n_rot_slabs:
                xs = (xs * cos + pltpu.roll(xs, ROT_DIM // 2, 1) * sin_hi
                      + pltpu.roll(xs, LANES - ROT_DIM // 2, 1) * sin_lo)
            if d == 1:
                out_ref[0, 0, :, c * LANES:(c + 1) * LANES] = xs.astype(BF16)
            else:
                slab_ref[c] = xs
        if d == DEINTERLEAVE_STRIDE:
            for c in range(n_slabs):
                for r in range(d):
                    rows = slab_ref[c, pl.ds(r, n, stride=d), :]
                    out_ref[0, r, :, c * LANES:(c + 1) * LANES] = rows.astype(BF16)
        elif d > 1:
            st = DEINTERLEAVE_STRIDE
            quarter = TOKEN_TILE // st
            for c in range(n_slabs):
                for r0 in range(st):
                    tmp_ref[c, r0 * quarter:(r0 + 1) * quarter, :] = slab_ref[c, pl.ds(r0, quarter, stride=st), :]
                for r1 in range(d // st):
                    for r0 in range(st):
                        rows = tmp_ref[c, pl.ds(r0 * quarter + r1, n, stride=st), :]
                        out_ref[0, st * r1 + r0, :, c * LANES:(c + 1) * LANES] = rows.astype(BF16)
    res = jnp.dot(h, w_ref[:, 3 * A_WIDTH:], preferred_element_type=F32)
    b_ref[0] = res.astype(BF16)


def _inproj(x, pre_g, w_proj, cos_t, sin_lo_t, sin_hi_t):
    bsz, seq, _ = x.shape
    nblk = seq // TOKEN_TILE
    tab_spec = pl.BlockSpec((TOKEN_TILE, LANES), lambda b, i: (i, 0))
    out_shapes = [jax.ShapeDtypeStruct((bsz, d, seq // d, A_WIDTH), BF16) for d in DILATIONS]
    out_specs = [pl.BlockSpec((1, d, TOKEN_TILE // d, A_WIDTH), lambda b, i: (b, 0, i, 0)) for d in DILATIONS]
    out_shapes.append(jax.ShapeDtypeStruct((bsz, seq, 3 * B_WIDTH), BF16))
    out_specs.append(pl.BlockSpec((1, TOKEN_TILE, 3 * B_WIDTH), lambda b, i: (b, i, 0)))
    return pl.pallas_call(
        _inproj_kernel,
        grid=(bsz, nblk),
        in_specs=[
            pl.BlockSpec((1, TOKEN_TILE, D_MODEL), lambda b, i: (b, i, 0)),
            _const_spec((1, D_MODEL)),
            _const_spec((D_MODEL, 3 * A_WIDTH + 3 * B_WIDTH)),
            tab_spec, tab_spec, tab_spec,
        ],
        out_specs=out_specs,
        out_shape=out_shapes,
        scratch_shapes=[pltpu.VMEM((A_WIDTH // LANES, TOKEN_TILE, LANES), F32)] * 2,
        compiler_params=_params(),
    )(x, pre_g, w_proj, cos_t, sin_lo_t, sin_hi_t)


def _head_stack(q, n_heads):
    rows, width = q.shape
    lane_head = lax.broadcasted_iota(jnp.int32, (rows, width), 1) // HEAD_DIM
    zero = jnp.zeros_like(q)
    return jnp.concatenate([jnp.where(lane_head == hh, q, zero) for hh in range(n_heads)], axis=0)


def _softmax_parts(s):
    m = jnp.max(s, axis=-1, keepdims=True)
    p = jnp.exp(s - m)
    l = jnp.sum(p, axis=-1, keepdims=True)
    return p.astype(BF16), 1.0 / l, m + jnp.log(l)


def _band_bias():
    r = BAND_RADIUS
    q_off = np.arange(BAND_Q_TILE)[:, None]
    col = np.arange(BAND_Q_TILE + 2 * r)[None, :]
    band = (col >= q_off) & (col <= q_off + 2 * r)
    variants = []
    for v in range(4):
        ok = band
        if v & 1:
            ok = ok & (col >= r)
        if v & 2:
            ok = ok & (col < BAND_Q_TILE + r)
        variants.append(np.where(ok, 0.0, NEG).astype(np.float32))
    return jnp.asarray(np.stack(variants))


def _attn_a_kernel(q_ref, kp_ref, kc_ref, kn_ref, vp_ref, vc_ref, vn_ref, bias_ref, o_ref, lse_ref, k_ext, v_ext,
                   *, q_block):
    i = pl.program_id(1)
    last = pl.num_programs(1) - 1
    r = BAND_RADIUS
    k_ext[0:r] = kp_ref[0]
    k_ext[r:r + q_block] = kc_ref[0]
    k_ext[r + q_block:] = kn_ref[0]
    v_ext[0:r] = vp_ref[0]
    v_ext[r:r + q_block] = vc_ref[0]
    v_ext[r + q_block:] = vn_ref[0]
    n_keys = BAND_Q_TILE + 2 * r
    n_tiles = q_block // BAND_Q_TILE

    def scores(j):
        q = _head_stack(q_ref[0, j * BAND_Q_TILE:(j + 1) * BAND_Q_TILE, :], A_GROUP_HEADS)
        k = k_ext[j * BAND_Q_TILE:j * BAND_Q_TILE + n_keys, :]
        return lax.dot_general(q, k, (((1,), (1,)), ((), ())), preferred_element_type=F32)

    s_next = scores(0)
    for j in range(n_tiles):
        variant = 0
        if j == 0:
            variant = variant + jnp.where(i == 0, 1, 0)
        if j == n_tiles - 1:
            variant = variant + jnp.where(i == last, 2, 0)
        v = v_ext[j * BAND_Q_TILE:j * BAND_Q_TILE + n_keys, :]
        s = s_next
        if j + 1 < n_tiles:
            s_next = scores(j + 1)
        sl = slice(j * BAND_Q_TILE, (j + 1) * BAND_Q_TILE)
        probs, scales = [], []
        for hh in range(A_GROUP_HEADS):
            p, inv_l, lse_h = _softmax_parts(s[hh * BAND_Q_TILE:(hh + 1) * BAND_Q_TILE] + bias_ref[variant])
            probs.append(p)
            scales.append(inv_l)
            lse_ref[0, sl, hh * HEAD_DIM:(hh + 1) * HEAD_DIM] = jnp.broadcast_to(lse_h, (BAND_Q_TILE, HEAD_DIM))
        o = jnp.dot(jnp.concatenate(probs, axis=0), v, preferred_element_type=F32)
        for hh in range(A_GROUP_HEADS):
            lanes = slice(hh * HEAD_DIM, (hh + 1) * HEAD_DIM)
            o_ref[0, sl, lanes] = o[hh * BAND_Q_TILE:(hh + 1) * BAND_Q_TILE, lanes] * scales[hh]


def _attn_a(qkv, band_bias):
    n_seq, seq_len, _ = qkv.shape
    q_block = min(BAND_Q_BLOCK, seq_len)
    halo_per_block = q_block // BAND_RADIUS
    n_halo = seq_len // BAND_RADIUS

    def cur(col):
        return pl.BlockSpec((1, q_block, A_GROUP_WIDTH), lambda s, i: (s, i, col))

    def prev(col):
        return pl.BlockSpec((1, BAND_RADIUS, A_GROUP_WIDTH),
                            lambda s, i: (s, jnp.maximum(i * halo_per_block - 1, 0), col))

    def nxt(col):
        return pl.BlockSpec((1, BAND_RADIUS, A_GROUP_WIDTH),
                            lambda s, i: (s, jnp.minimum((i + 1) * halo_per_block, n_halo - 1), col))

    out_spec = pl.BlockSpec((1, q_block, A_GROUP_WIDTH), lambda s, i: (s, i, 0))
    out_shape = jax.ShapeDtypeStruct((n_seq, seq_len, A_GROUP_WIDTH), F32)
    ext = pltpu.VMEM((q_block + 2 * BAND_RADIUS, A_GROUP_WIDTH), BF16)
    return pl.pallas_call(
        functools.partial(_attn_a_kernel, q_block=q_block),
        grid=(n_seq, seq_len // q_block),
        in_specs=[cur(0), prev(1), cur(1), nxt(1), prev(2), cur(2), nxt(2), _const_spec(band_bias.shape)],
        out_specs=[out_spec, out_spec],
        out_shape=[out_shape, out_shape],
        scratch_shapes=[ext, ext],
        compiler_params=_params(),
    )(qkv, qkv, qkv, qkv, qkv, qkv, qkv, band_bias)


def _attn_b_kernel(q_ref, kp_ref, kc_ref, kn_ref, vp_ref, vc_ref, vn_ref, bias_ref, o_ref, k_ext, v_ext):
    i = pl.program_id(1)
    last = pl.num_programs(1) - 1
    halo = (NA_ROWS // 2) * GRID_W
    step_tokens = B_ROWS_PER_STEP * GRID_W
    win = NA_ROWS * GRID_W
    k_ext[0:halo] = kp_ref[0]
    k_ext[halo:halo + step_tokens] = kc_ref[0]
    k_ext[halo + step_tokens:] = kn_ref[0]
    v_ext[0:halo] = vp_ref[0]
    v_ext[halo:halo + step_tokens] = vc_ref[0]
    v_ext[halo + step_tokens:] = vn_ref[0]
    lo = jnp.where(i == 0, NA_ROWS // 2, 0)
    hi = jnp.where(i == last, B_ROWS_PER_STEP - NA_ROWS // 2, B_ROWS_PER_STEP - 1)
    n_halves = B_HEADS // B_HALF_HEADS
    units = [(rr, half) for rr in range(B_ROWS_PER_STEP) for half in range(n_halves)]

    def window(rr):
        w_row = jnp.clip(rr, lo, hi)
        start = pl.multiple_of(w_row * GRID_W, GRID_W)
        delta = w_row - rr + (NA_ROWS // 2 - 1)
        return start, delta

    def scores(rr, half):
        start, _ = window(rr)
        lanes = slice(half * B_HALF_WIDTH, (half + 1) * B_HALF_WIDTH)
        q = _head_stack(q_ref[0, rr * GRID_W:(rr + 1) * GRID_W, lanes], B_HALF_HEADS)
        k = k_ext[pl.ds(start, win), lanes]
        return lax.dot_general(q, k, (((1,), (1,)), ((), ())), preferred_element_type=F32)

    s_next = scores(*units[0])
    for u, (rr, half) in enumerate(units):
        start, delta = window(rr)
        rows = slice(rr * GRID_W, (rr + 1) * GRID_W)
        lanes = slice(half * B_HALF_WIDTH, (half + 1) * B_HALF_WIDTH)
        s = s_next
        if u + 1 < len(units):
            s_next = scores(*units[u + 1])
        probs, scales = [], []
        for hh in range(B_HALF_HEADS):
            b_rows = pl.ds((half * B_HALF_HEADS + hh) * GRID_W, GRID_W)
            p, inv_l, _ = _softmax_parts(s[hh * GRID_W:(hh + 1) * GRID_W] + bias_ref[delta, b_rows, :])
            probs.append(p)
            scales.append(inv_l)
        v = v_ext[pl.ds(start, win), lanes]
        o = jnp.dot(jnp.concatenate(probs, axis=0), v, preferred_element_type=F32)
        for hh in range(B_HALF_HEADS):
            h_lanes = slice(hh * HEAD_DIM, (hh + 1) * HEAD_DIM)
            o_h = o[hh * GRID_W:(hh + 1) * GRID_W, h_lanes] * scales[hh]
            col = half * B_HALF_WIDTH + hh * HEAD_DIM
            o_ref[0, rows, col:col + HEAD_DIM] = o_h.astype(BF16)


def _attn_b(qkv, bias):
    bsz, seq, _ = qkv.shape
    step_tokens = B_ROWS_PER_STEP * GRID_W
    halo = (NA_ROWS // 2) * GRID_W
    per = step_tokens // halo
    n_halo = seq // halo

    def cur(col):
        return pl.BlockSpec((1, step_tokens, B_WIDTH), lambda b, i: (b, i, col))

    def prev(col):
        return pl.BlockSpec((1, halo, B_WIDTH), lambda b, i: (b, jnp.maximum(i * per - 1, 0), col))

    def nxt(col):
        return pl.BlockSpec((1, halo, B_WIDTH), lambda b, i: (b, jnp.minimum((i + 1) * per, n_halo - 1), col))

    ext = pltpu.VMEM((step_tokens + 2 * halo, B_WIDTH), BF16)
    return pl.pallas_call(
        _attn_b_kernel,
        grid=(bsz, seq // step_tokens),
        in_specs=[cur(0), prev(1), cur(1), nxt(1), prev(2), cur(2), nxt(2),
                  _const_spec(bias.shape)],
        out_specs=pl.BlockSpec((1, step_tokens, B_WIDTH), lambda b, i: (b, i, 0)),
        out_shape=jax.ShapeDtypeStruct((bsz, seq, B_WIDTH), BF16),
        scratch_shapes=[ext, ext],
        compiler_params=_params(),
    )(qkv, qkv, qkv, qkv, qkv, qkv, qkv, bias)


MIX_SUB = 256


def _mixout_kernel(x_ref, o0_ref, l0_ref, o1_ref, l1_ref, o2_ref, l2_ref, ob_ref,
                   pre_ref, post_ref, wgate_ref, bgate_ref, wa_ref, wb_ref, wout_ref,
                   y_ref, nat_ref, tmp_ref):
    n_slabs = A_GROUP_WIDTH // LANES
    for k, (d, src) in enumerate(((DILATIONS[1], o1_ref), (DILATIONS[1], l1_ref),
                                  (DILATIONS[2], o2_ref), (DILATIONS[2], l2_ref))):
        n = TOKEN_TILE // d
        st = DEINTERLEAVE_STRIDE
        quarter = TOKEN_TILE // st
        for c in range(n_slabs):
            dst = k * n_slabs + c
            if d == st:
                for r in range(d):
                    nat_ref[dst, pl.ds(r, n, stride=d), :] = src[0, r, :, c * LANES:(c + 1) * LANES]
            else:
                tmp = dst - 2 * n_slabs
                for r1 in range(d // st):
                    for r0 in range(st):
                        tmp_ref[tmp, pl.ds(r0 * quarter + r1, n, stride=st), :] = \
                            src[0, st * r1 + r0, :, c * LANES:(c + 1) * LANES]
                for r0 in range(st):
                    nat_ref[dst, pl.ds(r0, quarter, stride=st), :] = tmp_ref[tmp, r0 * quarter:(r0 + 1) * quarter, :]

    def nat(k, sl):
        return jnp.concatenate([nat_ref[k * n_slabs + c, sl, :] for c in range(n_slabs)], axis=-1)

    for t in range(TOKEN_TILE // MIX_SUB):
        sl = slice(t * MIX_SUB, (t + 1) * MIX_SUB)
        x = x_ref[0, sl, :]
        h = _rms(x, pre_ref[...]).astype(BF16)
        gates = jax.nn.sigmoid(jnp.dot(h, wgate_ref[...], preferred_element_type=F32) + bgate_ref[...])
        o0, l0 = o0_ref[0, 0, sl, :], l0_ref[0, 0, sl, :]
        o1, l1, o2, l2 = nat(0, sl), nat(1, sl), nat(2, sl), nat(3, sl)
        mx = jnp.maximum(jnp.maximum(l0, l1), l2)
        e0, e1, e2 = jnp.exp(l0 - mx), jnp.exp(l1 - mx), jnp.exp(l2 - mx)
        oa = (e0 * o0 + e1 * o1 + e2 * o2) / (e0 + e1 + e2)
        ya = jnp.dot(oa.astype(BF16), wa_ref[...], preferred_element_type=F32)
        yb = jnp.dot(ob_ref[0, sl, :], wb_ref[...], preferred_element_type=F32)
        mixed = (gates[:, :D_MODEL] * ya + gates[:, D_MODEL:] * yb).astype(BF16)
        m = jnp.dot(mixed, wout_ref[...], preferred_element_type=F32)
        y_ref[0, sl, :] = x + _rms(m, post_ref[...])


def _mixout(x, oas, lses, ob, pre_g, post_g, w_gate, b_gate, w_a, w_b, w_out):
    bsz, seq, _ = x.shape
    nblk = seq // TOKEN_TILE
    tile_spec = pl.BlockSpec((1, TOKEN_TILE, D_MODEL), lambda b, i: (b, i, 0))
    in_specs = [tile_spec]
    operands = [x]
    for d, o, l in zip(DILATIONS, oas, lses):
        spec = pl.BlockSpec((1, d, TOKEN_TILE // d, A_GROUP_WIDTH), lambda b, i: (b, 0, i, 0))
        in_specs += [spec, spec]
        operands += [o, l]
    in_specs.append(pl.BlockSpec((1, TOKEN_TILE, B_WIDTH), lambda b, i: (b, i, 0)))
    operands.append(ob)
    for w in (pre_g, post_g, w_gate, b_gate, w_a, w_b, w_out):
        in_specs.append(_const_spec(w.shape))
        operands.append(w)
    return pl.pallas_call(
        _mixout_kernel,
        grid=(bsz, nblk),
        in_specs=in_specs,
        out_specs=tile_spec,
        out_shape=jax.ShapeDtypeStruct((bsz, seq, D_MODEL), F32),
        scratch_shapes=[pltpu.VMEM((4 * A_GROUP_WIDTH // LANES, TOKEN_TILE, LANES), F32),
                        pltpu.VMEM((2 * A_GROUP_WIDTH // LANES, TOKEN_TILE, LANES), F32)],
        compiler_params=_params(),
    )(*operands)


def _rotary_tables(seq):
    half = ROT_DIM // 2
    inv_freq = ROPE_THETA ** (-jnp.arange(0, ROT_DIM, 2, dtype=F32) / ROT_DIM)
    ang = jnp.arange(seq, dtype=F32)[:, None] * inv_freq[None, :]
    cos, sin = jnp.cos(ang), jnp.sin(ang)
    head_lane = np.arange(LANES) % HEAD_DIM
    place_cos = np.zeros((half, LANES), np.float32)
    place_lo = np.zeros((half, LANES), np.float32)
    place_hi = np.zeros((half, LANES), np.float32)
    for lane, hl in enumerate(head_lane):
        if hl < ROT_DIM:
            place_cos[hl % half, lane] = 1.0
            if hl < half:
                place_lo[hl, lane] = -1.0
            else:
                place_hi[hl - half, lane] = 1.0
    passthrough = jnp.asarray((head_lane >= ROT_DIM).astype(np.float32))[None, :]

    def place(values, onehot):
        return jnp.dot(values, jnp.asarray(onehot), precision=lax.Precision.HIGHEST)

    return place(cos, place_cos) + passthrough, place(sin, place_lo), place(sin, place_hi)


def _neighbourhood_bias(rpb):
    cj = np.arange(GRID_W)
    cs = np.clip(cj - NA_COLS // 2, 0, GRID_W - NA_COLS)
    col_valid = (cj[None, :] >= cs[:, None]) & (cj[None, :] < cs[:, None] + NA_COLS)
    dc_idx = np.clip(cj[None, :] - cj[:, None] + (NA_COLS - 1), 0, 2 * NA_COLS - 2)
    n_dc = 2 * NA_COLS - 1
    onehot = (dc_idx[None] == np.arange(n_dc)[:, None, None]).astype(np.float32)
    base = jnp.einsum("hrd,dqk->hrqk", rpb.astype(F32), jnp.asarray(onehot), precision=lax.Precision.HIGHEST)
    base = jnp.where(col_valid[None, None], base, NEG)
    tables = []
    for first in range(NA_ROWS):
        t = base[:, first:first + NA_ROWS].transpose(0, 2, 1, 3)
        tables.append(t.reshape(B_HEADS * GRID_W, NA_ROWS * GRID_W))
    return jnp.stack(tables, axis=0)


def _split_w_in(w_in):
    scale = HEAD_DIM ** -0.5
    o = 0
    qa = w_in[:, o:o + A_WIDTH] * scale; o += A_WIDTH
    ka = w_in[:, o:o + A_WIDTH]; o += A_WIDTH
    va = w_in[:, o:o + A_WIDTH]; o += A_WIDTH
    qb = w_in[:, o:o + B_WIDTH] * scale; o += B_WIDTH
    kb = w_in[:, o:o + B_WIDTH]; o += B_WIDTH
    vb = w_in[:, o:o + B_WIDTH]; o += B_WIDTH
    w_gate = w_in[:, o:]
    cols = []
    for g in range(len(DILATIONS)):
        sl = slice(g * A_GROUP_WIDTH, (g + 1) * A_GROUP_WIDTH)
        cols += [qa[:, sl], ka[:, sl], va[:, sl]]
    cols += [qb, kb, vb]
    return jnp.concatenate(cols, axis=1).astype(BF16), w_gate.astype(BF16)


def _encoder(x, p, tables):
    bsz, seq, _ = x.shape
    row = lambda v: v.reshape(1, -1)
    x1 = _ffn(x.reshape(bsz * seq, D_MODEL), row(p["ffn1_pre_g"]), row(p["ffn1_post_g"]),
              p["ffn1_wg"], p["ffn1_wu"], p["ffn1_wd"]).reshape(bsz, seq, D_MODEL)
    cos_t, sin_lo_t, sin_hi_t = tables
    a0, a1, a2, qkvb = _inproj(x1, row(p["mix_pre_g"]), p["w_proj"], cos_t, sin_lo_t, sin_hi_t)
    oas, lses = [], []
    band_bias = _band_bias()
    for d, qkv in zip(DILATIONS, (a0, a1, a2)):
        o, lse = _attn_a(qkv.reshape(bsz * d, seq // d, A_WIDTH), band_bias)
        oas.append(o.reshape(bsz, d, seq // d, A_GROUP_WIDTH))
        lses.append(lse.reshape(bsz, d, seq // d, A_GROUP_WIDTH))
    ob = _attn_b(qkvb, p["bias_b"])
    x2 = _mixout(x1, oas, lses, ob, row(p["mix_pre_g"]), row(p["mix_post_g"]), p["w_gate"], row(p["b_gate"]),
                 p["w_a"], p["w_b"], p["w_out"])
    x3 = _ffn(x2.reshape(bsz * seq, D_MODEL), row(p["ffn2_pre_g"]), row(p["ffn2_post_g"]),
              p["ffn2_wg"], p["ffn2_wu"], p["ffn2_wd"])
    return x3.reshape(bsz, seq, D_MODEL)


def kernel(x_prompt, x_sample, ffn1_pre_g, ffn1_post_g, ffn1_w_gate, ffn1_w_up, ffn1_w_down, mix_pre_g, mix_post_g,
           w_in, b_gate, rpb, w_branch_a, w_branch_b, w_out, ffn2_pre_g, ffn2_post_g, ffn2_w_gate, ffn2_w_up,
           ffn2_w_down):
    w_proj, w_gate = _split_w_in(w_in[0])
    p = {
        "ffn1_pre_g": ffn1_pre_g[0], "ffn1_post_g": ffn1_post_g[0],
        "ffn1_wg": ffn1_w_gate[0].astype(BF16), "ffn1_wu": ffn1_w_up[0].astype(BF16),
        "ffn1_wd": ffn1_w_down[0].astype(BF16),
        "mix_pre_g": mix_pre_g[0], "mix_post_g": mix_post_g[0],
        "w_proj": w_proj, "w_gate": w_gate, "b_gate": b_gate[0],
        "bias_b": _neighbourhood_bias(rpb[0]),
        "w_a": w_branch_a[0].astype(BF16), "w_b": w_branch_b[0].astype(BF16), "w_out": w_out[0].astype(BF16),
        "ffn2_pre_g": ffn2_pre_g[0], "ffn2_post_g": ffn2_post_g[0],
        "ffn2_wg": ffn2_w_gate[0].astype(BF16), "ffn2_wu": ffn2_w_up[0].astype(BF16),
        "ffn2_wd": ffn2_w_down[0].astype(BF16),
    }
    tables = _rotary_tables(max(x_prompt.shape[1], x_sample.shape[1]))
    return _encoder(x_prompt, p, tables), _encoder(x_sample, p, tables)
```

```python
import functools

import jax
import jax.numpy as jnp
import numpy as np
from jax import lax
from jax.experimental import pallas as pl
from jax.experimental.pallas import tpu as pltpu

D_MODEL = 1024
D_FF = 2816
HEAD_DIM = 64
DILATIONS = (1, 4, 16)
BAND_RADIUS = 64
A_GROUP_HEADS = 4
A_GROUP_WIDTH = A_GROUP_HEADS * HEAD_DIM
A_WIDTH = 3 * A_GROUP_WIDTH
B_HEADS = 8
B_WIDTH = B_HEADS * HEAD_DIM
GRID_W = 64
NA_ROWS = 8
NA_COLS = 16
ROPE_THETA = 500000.0
ROT_DIM = HEAD_DIM // 4
RMS_EPS = 1e-6
NEG = -1e30

LANES = 128
VMEM_LIMIT = 56 * 1024 * 1024

TOKEN_TILE = 1024
DEINTERLEAVE_STRIDE = 4
FFN_TILE = 1024
FFN_SUB = 256
FF_CHUNKS = ((0, 1024), (1024, 1024), (2048, 768))
BAND_Q_BLOCK = 2048
BAND_Q_TILE = 128
B_ROWS_PER_STEP = 32
B_HALF_HEADS = 4
B_HALF_WIDTH = B_HALF_HEADS * HEAD_DIM

BF16 = jnp.bfloat16
F32 = jnp.float32


def _rms(x, g):
    return x * lax.rsqrt(jnp.mean(x * x, axis=-1, keepdims=True) + RMS_EPS) * g


def _const_spec(shape):
    zeros = (0,) * len(shape)
    return pl.BlockSpec(shape, lambda *_: zeros, pipeline_mode=pl.Buffered(1))


def _params():
    return pltpu.CompilerParams(vmem_limit_bytes=VMEM_LIMIT)


def _ffn_kernel(x_ref, pre_ref, post_ref, wg_ref, wu_ref, wd_ref, o_ref):
    n_sub = FFN_TILE // FFN_SUB

    def rows(t):
        return slice(t * FFN_SUB, (t + 1) * FFN_SUB)

    def pre(t):
        return _rms(x_ref[rows(t), :], pre_ref[...]).astype(BF16)

    def finish(t, y):
        o_ref[rows(t), :] = x_ref[rows(t), :] + 0.5 * _rms(y, post_ref[...])

    h = pre(0)
    y_prev = None
    for t in range(n_sub):
        y = None
        h_next = None
        for c, (start, size) in enumerate(FF_CHUNKS):
            g = jnp.dot(h, wg_ref[:, start:start + size], preferred_element_type=F32)
            u = jnp.dot(h, wu_ref[:, start:start + size], preferred_element_type=F32)
            a = (g * jax.nn.sigmoid(g) * u).astype(BF16)
            part = jnp.dot(a, wd_ref[start:start + size, :], preferred_element_type=F32)
            y = part if y is None else y + part
            if c == 0:
                if t + 1 < n_sub:
                    h_next = pre(t + 1)
                if t > 0:
                    finish(t - 1, y_prev)
        h, y_prev = h_next, y
    finish(n_sub - 1, y_prev)


def _ffn(x2d, pre_g, post_g, wg, wu, wd):
    m = x2d.shape[0]
    return pl.pallas_call(
        _ffn_kernel,
        grid=(m // FFN_TILE,),
        in_specs=[
            pl.BlockSpec((FFN_TILE, D_MODEL), lambda i: (i, 0)),
            _const_spec((1, D_MODEL)),
            _const_spec((1, D_MODEL)),
            _const_spec((D_MODEL, D_FF)),
            _const_spec((D_MODEL, D_FF)),
            _const_spec((D_FF, D_MODEL)),
        ],
        out_specs=pl.BlockSpec((FFN_TILE, D_MODEL), lambda i: (i, 0)),
        out_shape=jax.ShapeDtypeStruct((m, D_MODEL), F32),
        compiler_params=_params(),
    )(x2d, pre_g, post_g, wg, wu, wd)


def _inproj_kernel(x_ref, pre_ref, w_ref, rot_ref,
                   a0_ref, a1_ref, a2_ref, b_ref, slab_ref, tmp_ref):
    h = _rms(x_ref[0], pre_ref[...]).astype(BF16)
    cos = rot_ref[:, 0:LANES]
    sin_lo = rot_ref[:, LANES:2 * LANES]
    sin_hi = rot_ref[:, 2 * LANES:3 * LANES]
    n_slabs = A_WIDTH // LANES
    n_rot_slabs = 2 * A_GROUP_WIDTH // LANES
    for g, (d, out_ref) in enumerate(zip(DILATIONS, (a0_ref, a1_ref, a2_ref))):
        res = jnp.dot(h, w_ref[:, g * A_WIDTH:(g + 1) * A_WIDTH], preferred_element_type=F32)
        n = TOKEN_TILE // d
        for c in range(n_slabs):
            xs = res[:, c * LANES:(c + 1) * LANES]
            if c < n_rot_slabs:
                xs = (xs * cos + pltpu.roll(xs, ROT_DIM // 2, 1) * sin_hi
                      + pltpu.roll(xs, LANES - ROT_DIM // 2, 1) * sin_lo)
            if d == 1:
                out_ref[0, 0, :, c * LANES:(c + 1) * LANES] = xs.astype(BF16)
            else:
                slab_ref[c] = xs
        if d == DEINTERLEAVE_STRIDE:
            for c in range(n_slabs):
                for r in range(d):
                    rows = slab_ref[c, pl.ds(r, n, stride=d), :]
                    out_ref[0, r, :, c * LANES:(c + 1) * LANES] = rows.astype(BF16)
        elif d > 1:
            st = DEINTERLEAVE_STRIDE
            quarter = TOKEN_TILE // st
            for c in range(n_slabs):
                for r0 in range(st):
                    tmp_ref[c, r0 * quarter:(r0 + 1) * quarter, :] = slab_ref[c, pl.ds(r0, quarter, stride=st), :]
                for r1 in range(d // st):
                    for r0 in range(st):
                        rows = tmp_ref[c, pl.ds(r0 * quarter + r1, n, stride=st), :]
                        out_ref[0, st * r1 + r0, :, c * LANES:(c + 1) * LANES] = rows.astype(BF16)
    res = jnp.dot(h, w_ref[:, 3 * A_WIDTH:], preferred_element_type=F32)
    b_ref[0] = res.astype(BF16)


def _inproj(x, pre_g, w_proj, rot_tables):
    bsz, seq, _ = x.shape
    nblk = seq // TOKEN_TILE
    tab_spec = pl.BlockSpec((TOKEN_TILE, 3 * LANES), lambda b, i: (i, 0))
    out_shapes = [jax.ShapeDtypeStruct((bsz, d, seq // d, A_WIDTH), BF16) for d in DILATIONS]
    out_specs = [pl.BlockSpec((1, d, TOKEN_TILE // d, A_WIDTH), lambda b, i: (b, 0, i, 0)) for d in DILATIONS]
    out_shapes.append(jax.ShapeDtypeStruct((bsz, seq, 3 * B_WIDTH), BF16))
    out_specs.append(pl.BlockSpec((1, TOKEN_TILE, 3 * B_WIDTH), lambda b, i: (b, i, 0)))
    return pl.pallas_call(
        _inproj_kernel,
        grid=(bsz, nblk),
        in_specs=[
            pl.BlockSpec((1, TOKEN_TILE, D_MODEL), lambda b, i: (b, i, 0)),
            _const_spec((1, D_MODEL)),
            _const_spec((D_MODEL, 3 * A_WIDTH + 3 * B_WIDTH)),
            tab_spec,
        ],
        out_specs=out_specs,
        out_shape=out_shapes,
        scratch_shapes=[pltpu.VMEM((A_WIDTH // LANES, TOKEN_TILE, LANES), F32)] * 2,
        compiler_params=_params(),
    )(x, pre_g, w_proj, rot_tables)


def _head_stack(q, n_heads):
    rows, width = q.shape
    lane_head = lax.broadcasted_iota(jnp.int32, (rows, width), 1) // HEAD_DIM
    zero = jnp.zeros_like(q)
    return jnp.concatenate([jnp.where(lane_head == hh, q, zero) for hh in range(n_heads)], axis=0)


def _softmax_parts(s):
    m = jnp.max(s, axis=-1, keepdims=True)
    p = jnp.exp(s - m)
    l = jnp.sum(p, axis=-1, keepdims=True)
    return p.astype(BF16), 1.0 / l, m + jnp.log(l)


def _band_bias():
    r = BAND_RADIUS
    q_off = np.arange(BAND_Q_TILE)[:, None]
    col = np.arange(BAND_Q_TILE + 2 * r)[None, :]
    band = (col >= q_off) & (col <= q_off + 2 * r)
    variants = []
    for v in range(4):
        ok = band
        if v & 1:
            ok = ok & (col >= r)
        if v & 2:
            ok = ok & (col < BAND_Q_TILE + r)
        variants.append(np.where(ok, 0.0, NEG).astype(np.float32))
    return jnp.asarray(np.stack(variants))


def _attn_a_kernel(q_ref, kp_ref, kc_ref, kn_ref, vp_ref, vc_ref, vn_ref, bias_ref, ol_ref, k_ext, v_ext,
                   *, q_block):
    i = pl.program_id(1)
    last = pl.num_programs(1) - 1
    r = BAND_RADIUS
    k_ext[0:r] = kp_ref[0]
    k_ext[r:r + q_block] = kc_ref[0]
    k_ext[r + q_block:] = kn_ref[0]
    v_ext[0:r] = vp_ref[0]
    v_ext[r:r + q_block] = vc_ref[0]
    v_ext[r + q_block:] = vn_ref[0]
    n_keys = BAND_Q_TILE + 2 * r
    n_tiles = q_block // BAND_Q_TILE

    def scores(j):
        q = _head_stack(q_ref[0, j * BAND_Q_TILE:(j + 1) * BAND_Q_TILE, :], A_GROUP_HEADS)
        k = k_ext[j * BAND_Q_TILE:j * BAND_Q_TILE + n_keys, :]
        return lax.dot_general(q, k, (((1,), (1,)), ((), ())), preferred_element_type=F32)

    s_next = scores(0)
    for j in range(n_tiles):
        variant = 0
        if j == 0:
            variant = variant + jnp.where(i == 0, 1, 0)
        if j == n_tiles - 1:
            variant = variant + jnp.where(i == last, 2, 0)
        v = v_ext[j * BAND_Q_TILE:j * BAND_Q_TILE + n_keys, :]
        s = s_next
        if j + 1 < n_tiles:
            s_next = scores(j + 1)
        sl = slice(j * BAND_Q_TILE, (j + 1) * BAND_Q_TILE)
        probs, scales = [], []
        for hh in range(A_GROUP_HEADS):
            p, inv_l, lse_h = _softmax_parts(s[hh * BAND_Q_TILE:(hh + 1) * BAND_Q_TILE] + bias_ref[variant])
            probs.append(p)
            scales.append(inv_l)
            lse_lanes = slice(A_GROUP_WIDTH + hh * HEAD_DIM, A_GROUP_WIDTH + (hh + 1) * HEAD_DIM)
            ol_ref[0, sl, lse_lanes] = jnp.broadcast_to(lse_h, (BAND_Q_TILE, HEAD_DIM))
        o = jnp.dot(jnp.concatenate(probs, axis=0), v, preferred_element_type=F32)
        for hh in range(A_GROUP_HEADS):
            lanes = slice(hh * HEAD_DIM, (hh + 1) * HEAD_DIM)
            ol_ref[0, sl, lanes] = o[hh * BAND_Q_TILE:(hh + 1) * BAND_Q_TILE, lanes] * scales[hh]


def _attn_a(qkv, band_bias):
    n_seq, seq_len, _ = qkv.shape
    q_block = min(BAND_Q_BLOCK, seq_len)
    halo_per_block = q_block // BAND_RADIUS
    n_halo = seq_len // BAND_RADIUS

    def cur(col):
        return pl.BlockSpec((1, q_block, A_GROUP_WIDTH), lambda s, i: (s, i, col))

    def prev(col):
        return pl.BlockSpec((1, BAND_RADIUS, A_GROUP_WIDTH),
                            lambda s, i: (s, jnp.maximum(i * halo_per_block - 1, 0), col))

    def nxt(col):
        return pl.BlockSpec((1, BAND_RADIUS, A_GROUP_WIDTH),
                            lambda s, i: (s, jnp.minimum((i + 1) * halo_per_block, n_halo - 1), col))

    ext = pltpu.VMEM((q_block + 2 * BAND_RADIUS, A_GROUP_WIDTH), BF16)
    return pl.pallas_call(
        functools.partial(_attn_a_kernel, q_block=q_block),
        grid=(n_seq, seq_len // q_block),
        in_specs=[cur(0), prev(1), cur(1), nxt(1), prev(2), cur(2), nxt(2), _const_spec(band_bias.shape)],
        out_specs=pl.BlockSpec((1, q_block, 2 * A_GROUP_WIDTH), lambda s, i: (s, i, 0)),
        out_shape=jax.ShapeDtypeStruct((n_seq, seq_len, 2 * A_GROUP_WIDTH), F32),
        scratch_shapes=[ext, ext],
        compiler_params=_params(),
    )(qkv, qkv, qkv, qkv, qkv, qkv, qkv, band_bias)


def _attn_b_kernel(q_ref, kp_ref, kc_ref, kn_ref, vp_ref, vc_ref, vn_ref, bias_ref, o_ref, k_ext, v_ext):
    i = pl.program_id(1)
    last = pl.num_programs(1) - 1
    halo = (NA_ROWS // 2) * GRID_W
    step_tokens = B_ROWS_PER_STEP * GRID_W
    win = NA_ROWS * GRID_W
    k_ext[0:halo] = kp_ref[0]
    k_ext[halo:halo + step_tokens] = kc_ref[0]
    k_ext[halo + step_tokens:] = kn_ref[0]
    v_ext[0:halo] = vp_ref[0]
    v_ext[halo:halo + step_tokens] = vc_ref[0]
    v_ext[halo + step_tokens:] = vn_ref[0]
    lo = jnp.where(i == 0, NA_ROWS // 2, 0)
    hi = jnp.where(i == last, B_ROWS_PER_STEP - NA_ROWS // 2, B_ROWS_PER_STEP - 1)
    n_halves = B_HEADS // B_HALF_HEADS
    units = [(rr, half) for rr in range(B_ROWS_PER_STEP) for half in range(n_halves)]

    def window(rr):
        w_row = jnp.clip(rr, lo, hi)
        start = pl.multiple_of(w_row * GRID_W, GRID_W)
        delta = w_row - rr + (NA_ROWS // 2 - 1)
        return start, delta

    def scores(rr, half):
        start, _ = window(rr)
        lanes = slice(half * B_HALF_WIDTH, (half + 1) * B_HALF_WIDTH)
        q = _head_stack(q_ref[0, rr * GRID_W:(rr + 1) * GRID_W, lanes], B_HALF_HEADS)
        k = k_ext[pl.ds(start, win), lanes]
        return lax.dot_general(q, k, (((1,), (1,)), ((), ())), preferred_element_type=F32)

    s_next = scores(*units[0])
    for u, (rr, half) in enumerate(units):
        start, delta = window(rr)
        rows = slice(rr * GRID_W, (rr + 1) * GRID_W)
        lanes = slice(half * B_HALF_WIDTH, (half + 1) * B_HALF_WIDTH)
        s = s_next
        if u + 1 < len(units):
            s_next = scores(*units[u + 1])
        probs, scales = [], []
        for hh in range(B_HALF_HEADS):
            b_rows = pl.ds((half * B_HALF_HEADS + hh) * GRID_W, GRID_W)
            p, inv_l, _ = _softmax_parts(s[hh * GRID_W:(hh + 1) * GRID_W] + bias_ref[delta, b_rows, :])
            probs.append(p)
            scales.append(inv_l)
        v = v_ext[pl.ds(start, win), lanes]
        o = jnp.dot(jnp.concatenate(probs, axis=0), v, preferred_element_type=F32)
        for hh in range(B_HALF_HEADS):
            h_lanes = slice(hh * HEAD_DIM, (hh + 1) * HEAD_DIM)
            o_h = o[hh * GRID_W:(hh + 1) * GRID_W, h_lanes] * scales[hh]
            col = half * B_HALF_WIDTH + hh * HEAD_DIM
            o_ref[0, rows, col:col + HEAD_DIM] = o_h.astype(BF16)


def _attn_b(qkv, bias):
    bsz, seq, _ = qkv.shape
    step_tokens = B_ROWS_PER_STEP * GRID_W
    halo = (NA_ROWS // 2) * GRID_W
    per = step_tokens // halo
    n_halo = seq // halo

    def cur(col):
        return pl.BlockSpec((1, step_tokens, B_WIDTH), lambda b, i: (b, i, col))

    def prev(col):
        return pl.BlockSpec((1, halo, B_WIDTH), lambda b, i: (b, jnp.maximum(i * per - 1, 0), col))

    def nxt(col):
        return pl.BlockSpec((1, halo, B_WIDTH), lambda b, i: (b, jnp.minimum((i + 1) * per, n_halo - 1), col))

    ext = pltpu.VMEM((step_tokens + 2 * halo, B_WIDTH), BF16)
    return pl.pallas_call(
        _attn_b_kernel,
        grid=(bsz, seq // step_tokens),
        in_specs=[cur(0), prev(1), cur(1), nxt(1), prev(2), cur(2), nxt(2),
                  _const_spec(bias.shape)],
        out_specs=pl.BlockSpec((1, step_tokens, B_WIDTH), lambda b, i: (b, i, 0)),
        out_shape=jax.ShapeDtypeStruct((bsz, seq, B_WIDTH), BF16),
        scratch_shapes=[ext, ext],
        compiler_params=_params(),
    )(qkv, qkv, qkv, qkv, qkv, qkv, qkv, bias)


MIX_SUB = 256


def _mixout_kernel(x_ref, ol0_ref, ol1_ref, ol2_ref, ob_ref,
                   pre_ref, post_ref, wgate_ref, bgate_ref, wa_ref, wb_ref, wout_ref,
                   y_ref, nat_ref, tmp_ref):
    n_slabs = 2 * A_GROUP_WIDTH // LANES
    half_slabs = n_slabs // 2
    for k, (d, src) in enumerate(((DILATIONS[1], ol1_ref), (DILATIONS[2], ol2_ref))):
        n = TOKEN_TILE // d
        st = DEINTERLEAVE_STRIDE
        quarter = TOKEN_TILE // st
        for c in range(n_slabs):
            dst = k * n_slabs + c
            if d == st:
                for r in range(d):
                    nat_ref[dst, pl.ds(r, n, stride=d), :] = src[0, r, :, c * LANES:(c + 1) * LANES]
            else:
                for r1 in range(d // st):
                    for r0 in range(st):
                        tmp_ref[c, pl.ds(r0 * quarter + r1, n, stride=st), :] = \
                            src[0, st * r1 + r0, :, c * LANES:(c + 1) * LANES]
                for r0 in range(st):
                    nat_ref[dst, pl.ds(r0, quarter, stride=st), :] = tmp_ref[c, r0 * quarter:(r0 + 1) * quarter, :]

    def nat(k, part, sl):
        first = k * n_slabs + part * half_slabs
        return jnp.concatenate([nat_ref[first + c, sl, :] for c in range(half_slabs)], axis=-1)

    for t in range(TOKEN_TILE // MIX_SUB):
        sl = slice(t * MIX_SUB, (t + 1) * MIX_SUB)
        x = x_ref[0, sl, :]
        h = _rms(x, pre_ref[...]).astype(BF16)
        gates = jax.nn.sigmoid(jnp.dot(h, wgate_ref[...], preferred_element_type=F32) + bgate_ref[...])
        o0, l0 = ol0_ref[0, 0, sl, :A_GROUP_WIDTH], ol0_ref[0, 0, sl, A_GROUP_WIDTH:]
        o1, l1, o2, l2 = nat(0, 0, sl), nat(0, 1, sl), nat(1, 0, sl), nat(1, 1, sl)
        mx = jnp.maximum(jnp.maximum(l0, l1), l2)
        e0, e1, e2 = jnp.exp(l0 - mx), jnp.exp(l1 - mx), jnp.exp(l2 - mx)
        oa = (e0 * o0 + e1 * o1 + e2 * o2) / (e0 + e1 + e2)
        ya = jnp.dot(oa.astype(BF16), wa_ref[...], preferred_element_type=F32)
        yb = jnp.dot(ob_ref[0, sl, :], wb_ref[...], preferred_element_type=F32)
        mixed = (gates[:, :D_MODEL] * ya + gates[:, D_MODEL:] * yb).astype(BF16)
        m = jnp.dot(mixed, wout_ref[...], preferred_element_type=F32)
        y_ref[0, sl, :] = x + _rms(m, post_ref[...])


def _mixout(x, ols, ob, pre_g, post_g, w_gate, b_gate, w_a, w_b, w_out):
    bsz, seq, _ = x.shape
    nblk = seq // TOKEN_TILE
    tile_spec = pl.BlockSpec((1, TOKEN_TILE, D_MODEL), lambda b, i: (b, i, 0))
    in_specs = [tile_spec]
    operands = [x]
    for d, ol in zip(DILATIONS, ols):
        in_specs.append(pl.BlockSpec((1, d, TOKEN_TILE // d, 2 * A_GROUP_WIDTH), lambda b, i: (b, 0, i, 0)))
        operands.append(ol)
    in_specs.append(pl.BlockSpec((1, TOKEN_TILE, B_WIDTH), lambda b, i: (b, i, 0)))
    operands.append(ob)
    for w in (pre_g, post_g, w_gate, b_gate, w_a, w_b, w_out):
        in_specs.append(_const_spec(w.shape))
        operands.append(w)
    return pl.pallas_call(
        _mixout_kernel,
        grid=(bsz, nblk),
        in_specs=in_specs,
        out_specs=tile_spec,
        out_shape=jax.ShapeDtypeStruct((bsz, seq, D_MODEL), F32),
        scratch_shapes=[pltpu.VMEM((4 * A_GROUP_WIDTH // LANES, TOKEN_TILE, LANES), F32),
                        pltpu.VMEM((2 * A_GROUP_WIDTH // LANES, TOKEN_TILE, LANES), F32)],
        compiler_params=_params(),
    )(*operands)


def _rotary_tables(seq):
    half = ROT_DIM // 2
    inv_freq = ROPE_THETA ** (-jnp.arange(0, ROT_DIM, 2, dtype=F32) / ROT_DIM)
    ang = jnp.arange(seq, dtype=F32)[:, None] * inv_freq[None, :]
    cos, sin = jnp.cos(ang), jnp.sin(ang)
    ones = jnp.ones((seq, HEAD_DIM - ROT_DIM), F32)
    zeros = jnp.zeros((seq, HEAD_DIM - ROT_DIM), F32)
    zh = jnp.zeros((seq, half), F32)
    reps = LANES // HEAD_DIM
    cos_t = [cos, cos, ones] * reps
    sin_lo_t = [-sin, zh, zeros] * reps
    sin_hi_t = [zh, sin, zeros] * reps
    return jnp.concatenate(cos_t + sin_lo_t + sin_hi_t, axis=1)


def _neighbourhood_bias(rpb):
    cj = np.arange(GRID_W)
    cs = np.clip(cj - NA_COLS // 2, 0, GRID_W - NA_COLS)
    col_valid = (cj[None, :] >= cs[:, None]) & (cj[None, :] < cs[:, None] + NA_COLS)
    dc_idx = np.clip(cj[None, :] - cj[:, None] + (NA_COLS - 1), 0, 2 * NA_COLS - 2)
    n_dc = 2 * NA_COLS - 1
    onehot = (dc_idx[None] == np.arange(n_dc)[:, None, None]).astype(np.float32)
    base = jnp.einsum("hrd,dqk->hrqk", rpb.astype(F32), jnp.asarray(onehot), precision=lax.Precision.HIGHEST)
    base = jnp.where(col_valid[None, None], base, NEG)
    tables = []
    for first in range(NA_ROWS):
        t = base[:, first:first + NA_ROWS].transpose(0, 2, 1, 3)
        tables.append(t.reshape(B_HEADS * GRID_W, NA_ROWS * GRID_W))
    return jnp.stack(tables, axis=0)


def _split_w_in(w_in):
    scale = HEAD_DIM ** -0.5
    o = 0
    qa = w_in[:, o:o + A_WIDTH] * scale; o += A_WIDTH
    ka = w_in[:, o:o + A_WIDTH]; o += A_WIDTH
    va = w_in[:, o:o + A_WIDTH]; o += A_WIDTH
    qb = w_in[:, o:o + B_WIDTH] * scale; o += B_WIDTH
    kb = w_in[:, o:o + B_WIDTH]; o += B_WIDTH
    vb = w_in[:, o:o + B_WIDTH]; o += B_WIDTH
    w_gate = w_in[:, o:]
    cols = []
    for g in range(len(DILATIONS)):
        sl = slice(g * A_GROUP_WIDTH, (g + 1) * A_GROUP_WIDTH)
        cols += [qa[:, sl], ka[:, sl], va[:, sl]]
    cols += [qb, kb, vb]
    return jnp.concatenate(cols, axis=1).astype(BF16), w_gate.astype(BF16)


def _encoder(x, p, tables):
    bsz, seq, _ = x.shape
    row = lambda v: v.reshape(1, -1)
    x1 = _ffn(x.reshape(bsz * seq, D_MODEL), row(p["ffn1_pre_g"]), row(p["ffn1_post_g"]),
              p["ffn1_wg"], p["ffn1_wu"], p["ffn1_wd"]).reshape(bsz, seq, D_MODEL)
    a0, a1, a2, qkvb = _inproj(x1, row(p["mix_pre_g"]), p["w_proj"], tables)
    ols = []
    band_bias = _band_bias()
    for d, qkv in zip(DILATIONS, (a0, a1, a2)):
        ol = _attn_a(qkv.reshape(bsz * d, seq // d, A_WIDTH), band_bias)
        ols.append(ol.reshape(bsz, d, seq // d, 2 * A_GROUP_WIDTH))
    ob = _attn_b(qkvb, p["bias_b"])
    x2 = _mixout(x1, ols, ob, row(p["mix_pre_g"]), row(p["mix_post_g"]), p["w_gate"], row(p["b_gate"]),
                 p["w_a"], p["w_b"], p["w_out"])
    x3 = _ffn(x2.reshape(bsz * seq, D_MODEL), row(p["ffn2_pre_g"]), row(p["ffn2_post_g"]),
              p["ffn2_wg"], p["ffn2_wu"], p["ffn2_wd"])
    return x3.reshape(bsz, seq, D_MODEL)


def kernel(x_prompt, x_sample, ffn1_pre_g, ffn1_post_g, ffn1_w_gate, ffn1_w_up, ffn1_w_down, mix_pre_g, mix_post_g,
           w_in, b_gate, rpb, w_branch_a, w_branch_b, w_out, ffn2_pre_g, ffn2_post_g, ffn2_w_gate, ffn2_w_up,
           ffn2_w_down):
    w_proj, w_gate = _split_w_in(w_in[0])
    p = {
        "ffn1_pre_g": ffn1_pre_g[0], "ffn1_post_g": ffn1_post_g[0],
        "ffn1_wg": ffn1_w_gate[0].astype(BF16), "ffn1_wu": ffn1_w_up[0].astype(BF16),
        "ffn1_wd": ffn1_w_down[0].astype(BF16),
        "mix_pre_g": mix_pre_g[0], "mix_post_g": mix_post_g[0],
        "w_proj": w_proj, "w_gate": w_gate, "b_gate": b_gate[0],
        "bias_b": _neighbourhood_bias(rpb[0]),
        "w_a": w_branch_a[0].astype(BF16), "w_b": w_branch_b[0].astype(BF16), "w_out": w_out[0].astype(BF16),
        "ffn2_pre_g": ffn2_pre_g[0], "ffn2_post_g": ffn2_post_g[0],
        "ffn2_wg": ffn2_w_gate[0].astype(BF16), "ffn2_wu": ffn2_w_up[0].astype(BF16),
        "ffn2_wd": ffn2_w_down[0].astype(BF16),
    }
    tables = _rotary_tables(max(x_prompt.shape[1], x_sample.shape[1]))
    return _encoder(x_prompt, p, tables), _encoder(x_sample, p, tables)
```

```python
import functools

import jax
import jax.numpy as jnp
import numpy as np
from jax import lax
from jax.experimental import pallas as pl
from jax.experimental.pallas import tpu as pltpu

D_MODEL = 1024
D_FF = 2816
HEAD_DIM = 64
DILATIONS = (1, 4, 16)
BAND_RADIUS = 64
A_GROUP_HEADS = 4
A_GROUP_WIDTH = A_GROUP_HEADS * HEAD_DIM
A_WIDTH = 3 * A_GROUP_WIDTH
B_HEADS = 8
B_WIDTH = B_HEADS * HEAD_DIM
GRID_W = 64
NA_ROWS = 8
NA_COLS = 16
ROPE_THETA = 500000.0
ROT_DIM = HEAD_DIM // 4
RMS_EPS = 1e-6
NEG = -1e30

LANES = 128
VMEM_LIMIT = 56 * 1024 * 1024

TOKEN_TILE = 1024
DEINTERLEAVE_STRIDE = 4
FFN_TILE = 1024
FFN_SUB = 256
FF_CHUNKS = ((0, 1024), (1024, 1024), (2048, 768))
BAND_Q_BLOCK = 4096
BAND_Q_TILE = 128
B_ROWS_PER_STEP = 32
B_HALF_HEADS = 4
B_HALF_WIDTH = B_HALF_HEADS * HEAD_DIM

BF16 = jnp.bfloat16
F32 = jnp.float32


def _rms(x, g):
    return x * lax.rsqrt(jnp.mean(x * x, axis=-1, keepdims=True) + RMS_EPS) * g


def _const_spec(shape):
    zeros = (0,) * len(shape)
    return pl.BlockSpec(shape, lambda *_: zeros, pipeline_mode=pl.Buffered(1))


def _params():
    return pltpu.CompilerParams(vmem_limit_bytes=VMEM_LIMIT)


def _ffn_kernel(x_ref, pre_ref, post_ref, wg_ref, wu_ref, wd_ref, o_ref):
    n_sub = FFN_TILE // FFN_SUB

    def rows(t):
        return slice(t * FFN_SUB, (t + 1) * FFN_SUB)

    def pre(t):
        return _rms(x_ref[rows(t), :], pre_ref[...]).astype(BF16)

    def finish(t, y):
        o_ref[rows(t), :] = x_ref[rows(t), :] + 0.5 * _rms(y, post_ref[...])

    h = pre(0)
    y_prev = None
    for t in range(n_sub):
        y = None
        h_next = None
        for c, (start, size) in enumerate(FF_CHUNKS):
            g = jnp.dot(h, wg_ref[:, start:start + size], preferred_element_type=F32)
            u = jnp.dot(h, wu_ref[:, start:start + size], preferred_element_type=F32)
            a = (g * jax.nn.sigmoid(g) * u).astype(BF16)
            part = jnp.dot(a, wd_ref[start:start + size, :], preferred_element_type=F32)
            y = part if y is None else y + part
            if c == 0:
                if t + 1 < n_sub:
                    h_next = pre(t + 1)
                if t > 0:
                    finish(t - 1, y_prev)
        h, y_prev = h_next, y
    finish(n_sub - 1, y_prev)


def _ffn(x2d, pre_g, post_g, wg, wu, wd):
    m = x2d.shape[0]
    return pl.pallas_call(
        _ffn_kernel,
        grid=(m // FFN_TILE,),
        in_specs=[
            pl.BlockSpec((FFN_TILE, D_MODEL), lambda i: (i, 0)),
            _const_spec((1, D_MODEL)),
            _const_spec((1, D_MODEL)),
            _const_spec((D_MODEL, D_FF)),
            _const_spec((D_MODEL, D_FF)),
            _const_spec((D_FF, D_MODEL)),
        ],
        out_specs=pl.BlockSpec((FFN_TILE, D_MODEL), lambda i: (i, 0)),
        out_shape=jax.ShapeDtypeStruct((m, D_MODEL), F32),
        compiler_params=_params(),
    )(x2d, pre_g, post_g, wg, wu, wd)


def _inproj_kernel(x_ref, pre_ref, w_ref, cos_ref, sin_lo_ref, sin_hi_ref,
                   a0_ref, a1_ref, a2_ref, b_ref, slab_ref, tmp_ref):
    h = _rms(x_ref[0], pre_ref[...]).astype(BF16)
    cos = cos_ref[...]
    sin_lo = sin_lo_ref[...]
    sin_hi = sin_hi_ref[...]
    n_slabs = A_WIDTH // LANES
    n_rot_slabs = 2 * A_GROUP_WIDTH // LANES
    for g, (d, out_ref) in enumerate(zip(DILATIONS, (a0_ref, a1_ref, a2_ref))):
        res = jnp.dot(h, w_ref[:, g * A_WIDTH:(g + 1) * A_WIDTH], preferred_element_type=F32)
        n = TOKEN_TILE // d
        for c in range(n_slabs):
            xs = res[:, c * LANES:(c + 1) * LANES]
            if c < n_rot_slabs:
                xs = (xs * cos + pltpu.roll(xs, ROT_DIM // 2, 1) * sin_hi
                      + pltpu.roll(xs, LANES - ROT_DIM // 2, 1) * sin_lo)
            if d == 1:
                out_ref[0, 0, :, c * LANES:(c + 1) * LANES] = xs.astype(BF16)
            else:
                slab_ref[c] = xs
        if d == DEINTERLEAVE_STRIDE:
            for c in range(n_slabs):
                for r in range(d):
                    rows = slab_ref[c, pl.ds(r, n, stride=d), :]
                    out_ref[0, r, :, c * LANES:(c + 1) * LANES] = rows.astype(BF16)
        elif d > 1:
            st = DEINTERLEAVE_STRIDE
            quarter = TOKEN_TILE // st
            for c in range(n_slabs):
                for r0 in range(st):
                    tmp_ref[c, r0 * quarter:(r0 + 1) * quarter, :] = slab_ref[c, pl.ds(r0, quarter, stride=st), :]
                for r1 in range(d // st):
                    for r0 in range(st):
                        rows = tmp_ref[c, pl.ds(r0 * quarter + r1, n, stride=st), :]
                        out_ref[0, st * r1 + r0, :, c * LANES:(c + 1) * LANES] = rows.astype(BF16)
    res = jnp.dot(h, w_ref[:, 3 * A_WIDTH:], preferred_element_type=F32)
    b_ref[0] = res.astype(BF16)


def _inproj(x, pre_g, w_proj, cos_t, sin_lo_t, sin_hi_t):
    bsz, seq, _ = x.shape
    nblk = seq // TOKEN_TILE
    tab_spec = pl.BlockSpec((TOKEN_TILE, LANES), lambda b, i: (i, 0))
    out_shapes = [jax.ShapeDtypeStruct((bsz, d, seq // d, A_WIDTH), BF16) for d in DILATIONS]
    out_specs = [pl.BlockSpec((1, d, TOKEN_TILE // d, A_WIDTH), lambda b, i: (b, 0, i, 0)) for d in DILATIONS]
    out_shapes.append(jax.ShapeDtypeStruct((bsz, seq, 3 * B_WIDTH), BF16))
    out_specs.append(pl.BlockSpec((1, TOKEN_TILE, 3 * B_WIDTH), lambda b, i: (b, i, 0)))
    return pl.pallas_call(
        _inproj_kernel,
        grid=(bsz, nblk),
        in_specs=[
            pl.BlockSpec((1, TOKEN_TILE, D_MODEL), lambda b, i: (b, i, 0)),
            _const_spec((1, D_MODEL)),
            _const_spec((D_MODEL, 3 * A_WIDTH + 3 * B_WIDTH)),
            tab_spec, tab_spec, tab_spec,
        ],
        out_specs=out_specs,
        out_shape=out_shapes,
        scratch_shapes=[pltpu.VMEM((A_WIDTH // LANES, TOKEN_TILE, LANES), F32)] * 2,
        compiler_params=_params(),
    )(x, pre_g, w_proj, cos_t, sin_lo_t, sin_hi_t)


def _head_stack(q, n_heads):
    rows, width = q.shape
    lane_head = lax.broadcasted_iota(jnp.int32, (rows, width), 1) // HEAD_DIM
    zero = jnp.zeros_like(q)
    return jnp.concatenate([jnp.where(lane_head == hh, q, zero) for hh in range(n_heads)], axis=0)


def _softmax_parts(s):
    m = jnp.max(s, axis=-1, keepdims=True)
    p = jnp.exp(s - m)
    l = jnp.sum(p, axis=-1, keepdims=True)
    return p.astype(BF16), 1.0 / l, m + jnp.log(l)


def _band_bias():
    r = BAND_RADIUS
    q_off = np.arange(BAND_Q_TILE)[:, None]
    col = np.arange(BAND_Q_TILE + 2 * r)[None, :]
    band = (col >= q_off) & (col <= q_off + 2 * r)
    variants = []
    for v in range(4):
        ok = band
        if v & 1:
            ok = ok & (col >= r)
        if v & 2:
            ok = ok & (col < BAND_Q_TILE + r)
        variants.append(np.where(ok, 0.0, NEG).astype(np.float32))
    return jnp.asarray(np.stack(variants))


def _attn_a_kernel(q_ref, kp_ref, kc_ref, kn_ref, vp_ref, vc_ref, vn_ref, bias_ref, ol_ref, k_ext, v_ext,
                   *, q_block):
    i = pl.program_id(1)
    last = pl.num_programs(1) - 1
    r = BAND_RADIUS
    k_ext[0:r] = kp_ref[0]
    k_ext[r:r + q_block] = kc_ref[0]
    k_ext[r + q_block:] = kn_ref[0]
    v_ext[0:r] = vp_ref[0]
    v_ext[r:r + q_block] = vc_ref[0]
    v_ext[r + q_block:] = vn_ref[0]
    n_keys = BAND_Q_TILE + 2 * r
    n_tiles = q_block // BAND_Q_TILE

    def scores(j):
        q = _head_stack(q_ref[0, j * BAND_Q_TILE:(j + 1) * BAND_Q_TILE, :], A_GROUP_HEADS)
        k = k_ext[j * BAND_Q_TILE:j * BAND_Q_TILE + n_keys, :]
        return lax.dot_general(q, k, (((1,), (1,)), ((), ())), preferred_element_type=F32)

    s_next = scores(0)
    for j in range(n_tiles):
        variant = 0
        if j == 0:
            variant = variant + jnp.where(i == 0, 1, 0)
        if j == n_tiles - 1:
            variant = variant + jnp.where(i == last, 2, 0)
        v = v_ext[j * BAND_Q_TILE:j * BAND_Q_TILE + n_keys, :]
        s = s_next
        if j + 1 < n_tiles:
            s_next = scores(j + 1)
        sl = slice(j * BAND_Q_TILE, (j + 1) * BAND_Q_TILE)
        probs, scales = [], []
        for hh in range(A_GROUP_HEADS):
            p, inv_l, lse_h = _softmax_parts(s[hh * BAND_Q_TILE:(hh + 1) * BAND_Q_TILE] + bias_ref[variant])
            probs.append(p)
            scales.append(inv_l)
            lse_lanes = slice(A_GROUP_WIDTH + hh * HEAD_DIM, A_GROUP_WIDTH + (hh + 1) * HEAD_DIM)
            ol_ref[0, sl, lse_lanes] = jnp.broadcast_to(lse_h, (BAND_Q_TILE, HEAD_DIM))
        o = jnp.dot(jnp.concatenate(probs, axis=0), v, preferred_element_type=F32)
        for hh in range(A_GROUP_HEADS):
            lanes = slice(hh * HEAD_DIM, (hh + 1) * HEAD_DIM)
            ol_ref[0, sl, lanes] = o[hh * BAND_Q_TILE:(hh + 1) * BAND_Q_TILE, lanes] * scales[hh]


def _attn_a(qkv, band_bias):
    n_seq, seq_len, _ = qkv.shape
    q_block = min(BAND_Q_BLOCK, seq_len)
    halo_per_block = q_block // BAND_RADIUS
    n_halo = seq_len // BAND_RADIUS

    def cur(col):
        return pl.BlockSpec((1, q_block, A_GROUP_WIDTH), lambda s, i: (s, i, col))

    def prev(col):
        return pl.BlockSpec((1, BAND_RADIUS, A_GROUP_WIDTH),
                            lambda s, i: (s, jnp.maximum(i * halo_per_block - 1, 0), col))

    def nxt(col):
        return pl.BlockSpec((1, BAND_RADIUS, A_GROUP_WIDTH),
                            lambda s, i: (s, jnp.minimum((i + 1) * halo_per_block, n_halo - 1), col))

    ext = pltpu.VMEM((q_block + 2 * BAND_RADIUS, A_GROUP_WIDTH), BF16)
    return pl.pallas_call(
        functools.partial(_attn_a_kernel, q_block=q_block),
        grid=(n_seq, seq_len // q_block),
        in_specs=[cur(0), prev(1), cur(1), nxt(1), prev(2), cur(2), nxt(2), _const_spec(band_bias.shape)],
        out_specs=pl.BlockSpec((1, q_block, 2 * A_GROUP_WIDTH), lambda s, i: (s, i, 0)),
        out_shape=jax.ShapeDtypeStruct((n_seq, seq_len, 2 * A_GROUP_WIDTH), F32),
        scratch_shapes=[ext, ext],
        compiler_params=_params(),
    )(qkv, qkv, qkv, qkv, qkv, qkv, qkv, band_bias)


def _attn_b_kernel(q_ref, kp_ref, kc_ref, kn_ref, vp_ref, vc_ref, vn_ref, bias_ref, o_ref, k_ext, v_ext):
    i = pl.program_id(1)
    last = pl.num_programs(1) - 1
    halo = (NA_ROWS // 2) * GRID_W
    step_tokens = B_ROWS_PER_STEP * GRID_W
    win = NA_ROWS * GRID_W
    k_ext[0:halo] = kp_ref[0]
    k_ext[halo:halo + step_tokens] = kc_ref[0]
    k_ext[halo + step_tokens:] = kn_ref[0]
    v_ext[0:halo] = vp_ref[0]
    v_ext[halo:halo + step_tokens] = vc_ref[0]
    v_ext[halo + step_tokens:] = vn_ref[0]
    lo = jnp.where(i == 0, NA_ROWS // 2, 0)
    hi = jnp.where(i == last, B_ROWS_PER_STEP - NA_ROWS // 2, B_ROWS_PER_STEP - 1)
    n_halves = B_HEADS // B_HALF_HEADS
    units = [(rr, half) for rr in range(B_ROWS_PER_STEP) for half in range(n_halves)]

    def window(rr):
        w_row = jnp.clip(rr, lo, hi)
        start = pl.multiple_of(w_row * GRID_W, GRID_W)
        delta = w_row - rr + (NA_ROWS // 2 - 1)
        return start, delta

    def scores(rr, half):
        start, _ = window(rr)
        lanes = slice(half * B_HALF_WIDTH, (half + 1) * B_HALF_WIDTH)
        q = _head_stack(q_ref[0, rr * GRID_W:(rr + 1) * GRID_W, lanes], B_HALF_HEADS)
        k = k_ext[pl.ds(start, win), lanes]
        return lax.dot_general(q, k, (((1,), (1,)), ((), ())), preferred_element_type=F32)

    s_next = scores(*units[0])
    for u, (rr, half) in enumerate(units):
        start, delta = window(rr)
        rows = slice(rr * GRID_W, (rr + 1) * GRID_W)
        lanes = slice(half * B_HALF_WIDTH, (half + 1) * B_HALF_WIDTH)
        s = s_next
        if u + 1 < len(units):
            s_next = scores(*units[u + 1])
        probs, scales = [], []
        for hh in range(B_HALF_HEADS):
            b_rows = pl.ds((half * B_HALF_HEADS + hh) * GRID_W, GRID_W)
            p, inv_l, _ = _softmax_parts(s[hh * GRID_W:(hh + 1) * GRID_W] + bias_ref[delta, b_rows, :])
            probs.append(p)
            scales.append(inv_l)
        v = v_ext[pl.ds(start, win), lanes]
        o = jnp.dot(jnp.concatenate(probs, axis=0), v, preferred_element_type=F32)
        for hh in range(B_HALF_HEADS):
            h_lanes = slice(hh * HEAD_DIM, (hh + 1) * HEAD_DIM)
            o_h = o[hh * GRID_W:(hh + 1) * GRID_W, h_lanes] * scales[hh]
            col = half * B_HALF_WIDTH + hh * HEAD_DIM
            o_ref[0, rows, col:col + HEAD_DIM] = o_h.astype(BF16)


def _attn_b(qkv, bias):
    bsz, seq, _ = qkv.shape
    step_tokens = B_ROWS_PER_STEP * GRID_W
    halo = (NA_ROWS // 2) * GRID_W
    per = step_tokens // halo
    n_halo = seq // halo

    def cur(col):
        return pl.BlockSpec((1, step_tokens, B_WIDTH), lambda b, i: (b, i, col))

    def prev(col):
        return pl.BlockSpec((1, halo, B_WIDTH), lambda b, i: (b, jnp.maximum(i * per - 1, 0), col))

    def nxt(col):
        return pl.BlockSpec((1, halo, B_WIDTH), lambda b, i: (b, jnp.minimum((i + 1) * per, n_halo - 1), col))

    ext = pltpu.VMEM((step_tokens + 2 * halo, B_WIDTH), BF16)
    return pl.pallas_call(
        _attn_b_kernel,
        grid=(bsz, seq // step_tokens),
        in_specs=[cur(0), prev(1), cur(1), nxt(1), prev(2), cur(2), nxt(2),
                  _const_spec(bias.shape)],
        out_specs=pl.BlockSpec((1, step_tokens, B_WIDTH), lambda b, i: (b, i, 0)),
        out_shape=jax.ShapeDtypeStruct((bsz, seq, B_WIDTH), BF16),
        scratch_shapes=[ext, ext],
        compiler_params=_params(),
    )(qkv, qkv, qkv, qkv, qkv, qkv, qkv, bias)


MIX_SUB = 256


def _mixout_kernel(x_ref, ol0_ref, ol1_ref, ol2_ref, ob_ref,
                   pre_ref, post_ref, wgate_ref, bgate_ref, wa_ref, wb_ref, wout_ref,
                   y_ref, nat_ref, tmp_ref):
    n_slabs = 2 * A_GROUP_WIDTH // LANES
    half_slabs = n_slabs // 2
    for k, (d, src) in enumerate(((DILATIONS[1], ol1_ref), (DILATIONS[2], ol2_ref))):
        n = TOKEN_TILE // d
        st = DEINTERLEAVE_STRIDE
        quarter = TOKEN_TILE // st
        for c in range(n_slabs):
            dst = k * n_slabs + c
            if d == st:
                for r in range(d):
                    nat_ref[dst, pl.ds(r, n, stride=d), :] = src[0, r, :, c * LANES:(c + 1) * LANES]
            else:
                for r1 in range(d // st):
                    for r0 in range(st):
                        tmp_ref[c, pl.ds(r0 * quarter + r1, n, stride=st), :] = \
                            src[0, st * r1 + r0, :, c * LANES:(c + 1) * LANES]
                for r0 in range(st):
                    nat_ref[dst, pl.ds(r0, quarter, stride=st), :] = tmp_ref[c, r0 * quarter:(r0 + 1) * quarter, :]

    def nat(k, part, sl):
        first = k * n_slabs + part * half_slabs
        return jnp.concatenate([nat_ref[first + c, sl, :] for c in range(half_slabs)], axis=-1)

    for t in range(TOKEN_TILE // MIX_SUB):
        sl = slice(t * MIX_SUB, (t + 1) * MIX_SUB)
        x = x_ref[0, sl, :]
        h = _rms(x, pre_ref[...]).astype(BF16)
        gates = jax.nn.sigmoid(jnp.dot(h, wgate_ref[...], preferred_element_type=F32) + bgate_ref[...])
        o0, l0 = ol0_ref[0, 0, sl, :A_GROUP_WIDTH], ol0_ref[0, 0, sl, A_GROUP_WIDTH:]
        o1, l1, o2, l2 = nat(0, 0, sl), nat(0, 1, sl), nat(1, 0, sl), nat(1, 1, sl)
        mx = jnp.maximum(jnp.maximum(l0, l1), l2)
        e0, e1, e2 = jnp.exp(l0 - mx), jnp.exp(l1 - mx), jnp.exp(l2 - mx)
        oa = (e0 * o0 + e1 * o1 + e2 * o2) / (e0 + e1 + e2)
        ya = jnp.dot(oa.astype(BF16), wa_ref[...], preferred_element_type=F32)
        yb = jnp.dot(ob_ref[0, sl, :], wb_ref[...], preferred_element_type=F32)
        mixed = (gates[:, :D_MODEL] * ya + gates[:, D_MODEL:] * yb).astype(BF16)
        m = jnp.dot(mixed, wout_ref[...], preferred_element_type=F32)
        y_ref[0, sl, :] = x + _rms(m, post_ref[...])


def _mixout(x, ols, ob, pre_g, post_g, w_gate, b_gate, w_a, w_b, w_out):
    bsz, seq, _ = x.shape
    nblk = seq // TOKEN_TILE
    tile_spec = pl.BlockSpec((1, TOKEN_TILE, D_MODEL), lambda b, i: (b, i, 0))
    in_specs = [tile_spec]
    operands = [x]
    for d, ol in zip(DILATIONS, ols):
        in_specs.append(pl.BlockSpec((1, d, TOKEN_TILE // d, 2 * A_GROUP_WIDTH), lambda b, i: (b, 0, i, 0)))
        operands.append(ol)
    in_specs.append(pl.BlockSpec((1, TOKEN_TILE, B_WIDTH), lambda b, i: (b, i, 0)))
    operands.append(ob)
    for w in (pre_g, post_g, w_gate, b_gate, w_a, w_b, w_out):
        in_specs.append(_const_spec(w.shape))
        operands.append(w)
    return pl.pallas_call(
        _mixout_kernel,
        grid=(bsz, nblk),
        in_specs=in_specs,
        out_specs=tile_spec,
        out_shape=jax.ShapeDtypeStruct((bsz, seq, D_MODEL), F32),
        scratch_shapes=[pltpu.VMEM((4 * A_GROUP_WIDTH // LANES, TOKEN_TILE, LANES), F32),
                        pltpu.VMEM((2 * A_GROUP_WIDTH // LANES, TOKEN_TILE, LANES), F32)],
        compiler_params=_params(),
    )(*operands)


def _rotary_tables(seq):
    half = ROT_DIM // 2
    inv_freq = ROPE_THETA ** (-jnp.arange(0, ROT_DIM, 2, dtype=F32) / ROT_DIM)
    ang = jnp.arange(seq, dtype=F32)[:, None] * inv_freq[None, :]
    cos, sin = jnp.cos(ang), jnp.sin(ang)
    ones = jnp.ones((seq, HEAD_DIM - ROT_DIM), F32)
    zeros = jnp.zeros((seq, HEAD_DIM - ROT_DIM), F32)
    zh = jnp.zeros((seq, half), F32)
    cos_h = jnp.concatenate([cos, cos, ones], axis=1)
    sin_lo_h = jnp.concatenate([-sin, zh, zeros], axis=1)
    sin_hi_h = jnp.concatenate([zh, sin, zeros], axis=1)
    reps = LANES // HEAD_DIM
    return tuple(jnp.tile(t, (1, reps)) for t in (cos_h, sin_lo_h, sin_hi_h))


def _neighbourhood_bias(rpb):
    cj = np.arange(GRID_W)
    cs = np.clip(cj - NA_COLS // 2, 0, GRID_W - NA_COLS)
    col_valid = (cj[None, :] >= cs[:, None]) & (cj[None, :] < cs[:, None] + NA_COLS)
    dc_idx = np.clip(cj[None, :] - cj[:, None] + (NA_COLS - 1), 0, 2 * NA_COLS - 2)
    n_dc = 2 * NA_COLS - 1
    onehot = (dc_idx[None] == np.arange(n_dc)[:, None, None]).astype(np.float32)
    base = jnp.einsum("hrd,dqk->hrqk", rpb.astype(F32), jnp.asarray(onehot), precision=lax.Precision.HIGHEST)
    base = jnp.where(col_valid[None, None], base, NEG)
    tables = []
    for first in range(NA_ROWS):
        t = base[:, first:first + NA_ROWS].transpose(0, 2, 1, 3)
        tables.append(t.reshape(B_HEADS * GRID_W, NA_ROWS * GRID_W))
    return jnp.stack(tables, axis=0)


def _split_w_in(w_in):
    scale = HEAD_DIM ** -0.5
    o = 0
    qa = w_in[:, o:o + A_WIDTH] * scale; o += A_WIDTH
    ka = w_in[:, o:o + A_WIDTH]; o += A_WIDTH
    va = w_in[:, o:o + A_WIDTH]; o += A_WIDTH
    qb = w_in[:, o:o + B_WIDTH] * scale; o += B_WIDTH
    kb = w_in[:, o:o + B_WIDTH]; o += B_WIDTH
    vb = w_in[:, o:o + B_WIDTH]; o += B_WIDTH
    w_gate = w_in[:, o:]
    cols = []
    for g in range(len(DILATIONS)):
        sl = slice(g * A_GROUP_WIDTH, (g + 1) * A_GROUP_WIDTH)
        cols += [qa[:, sl], ka[:, sl], va[:, sl]]
    cols += [qb, kb, vb]
    return jnp.concatenate(cols, axis=1).astype(BF16), w_gate.astype(BF16)


def _encoder(x, p, tables):
    bsz, seq, _ = x.shape
    row = lambda v: v.reshape(1, -1)
    x1 = _ffn(x.reshape(bsz * seq, D_MODEL), row(p["ffn1_pre_g"]), row(p["ffn1_post_g"]),
              p["ffn1_wg"], p["ffn1_wu"], p["ffn1_wd"]).reshape(bsz, seq, D_MODEL)
    a0, a1, a2, qkvb = _inproj(x1, row(p["mix_pre_g"]), p["w_proj"], *tables)
    ols = []
    band_bias = _band_bias()
    for d, qkv in zip(DILATIONS, (a0, a1, a2)):
        ol = _attn_a(qkv.reshape(bsz * d, seq // d, A_WIDTH), band_bias)
        ols.append(ol.reshape(bsz, d, seq // d, 2 * A_GROUP_WIDTH))
    ob = _attn_b(qkvb, p["bias_b"])
    x2 = _mixout(x1, ols, ob, row(p["mix_pre_g"]), row(p["mix_post_g"]), p["w_gate"], row(p["b_gate"]),
                 p["w_a"], p["w_b"], p["w_out"])
    x3 = _ffn(x2.reshape(bsz * seq, D_MODEL), row(p["ffn2_pre_g"]), row(p["ffn2_post_g"]),
              p["ffn2_wg"], p["ffn2_wu"], p["ffn2_wd"])
    return x3.reshape(bsz, seq, D_MODEL)


def kernel(x_prompt, x_sample, ffn1_pre_g, ffn1_post_g, ffn1_w_gate, ffn1_w_up, ffn1_w_down, mix_pre_g, mix_post_g,
           w_in, b_gate, rpb, w_branch_a, w_branch_b, w_out, ffn2_pre_g, ffn2_post_g, ffn2_w_gate, ffn2_w_up,
           ffn2_w_down):
    w_proj, w_gate = _split_w_in(w_in[0])
    p = {
        "ffn1_pre_g": ffn1_pre_g[0], "ffn1_post_g": ffn1_post_g[0],
        "ffn1_wg": ffn1_w_gate[0].astype(BF16), "ffn1_wu": ffn1_w_up[0].astype(BF16),
        "ffn1_wd": ffn1_w_down[0].astype(BF16),
        "mix_pre_g": mix_pre_g[0], "mix_post_g": mix_post_g[0],
        "w_proj": w_proj, "w_gate": w_gate, "b_gate": b_gate[0],
        "bias_b": _neighbourhood_bias(rpb[0]),
        "w_a": w_branch_a[0].astype(BF16), "w_b": w_branch_b[0].astype(BF16), "w_out": w_out[0].astype(BF16),
        "ffn2_pre_g": ffn2_pre_g[0], "ffn2_post_g": ffn2_post_g[0],
        "ffn2_wg": ffn2_w_gate[0].astype(BF16), "ffn2_wu": ffn2_w_up[0].astype(BF16),
        "ffn2_wd": ffn2_w_down[0].astype(BF16),
    }
    tables = _rotary_tables(max(x_prompt.shape[1], x_sample.shape[1]))
    return _encoder(x_prompt, p, tables), _encoder(x_sample, p, tables)
```

```python
import functools

import jax
import jax.numpy as jnp
import numpy as np
from jax import lax
from jax.experimental import pallas as pl
from jax.experimental.pallas import tpu as pltpu

D_MODEL = 1024
D_FF = 2816
HEAD_DIM = 64
DILATIONS = (1, 4, 16)
BAND_RADIUS = 64
A_GROUP_HEADS = 4
A_GROUP_WIDTH = A_GROUP_HEADS * HEAD_DIM
A_WIDTH = 3 * A_GROUP_WIDTH
B_HEADS = 8
B_WIDTH = B_HEADS * HEAD_DIM
GRID_W = 64
NA_ROWS = 8
NA_COLS = 16
ROPE_THETA = 500000.0
ROT_DIM = HEAD_DIM // 4
RMS_EPS = 1e-6
NEG = -1e30

LANES = 128
VMEM_LIMIT = 56 * 1024 * 1024

TOKEN_TILE = 1024
DEINTERLEAVE_STRIDE = 4
FFN_TILE = 1024
FFN_SUB = 256
FF_CHUNKS = ((0, 1024), (1024, 1024), (2048, 768))
BAND_Q_BLOCK = 4096
BAND_Q_TILE = 128
LSE_LANES = LANES // A_GROUP_HEADS
OL_WIDTH = A_GROUP_WIDTH + LANES
B_ROWS_PER_STEP = 32
B_HALF_HEADS = 4
B_HALF_WIDTH = B_HALF_HEADS * HEAD_DIM

BF16 = jnp.bfloat16
F32 = jnp.float32


def _rms(x, g):
    return x * lax.rsqrt(jnp.mean(x * x, axis=-1, keepdims=True) + RMS_EPS) * g


def _const_spec(shape):
    zeros = (0,) * len(shape)
    return pl.BlockSpec(shape, lambda *_: zeros, pipeline_mode=pl.Buffered(1))


def _params():
    return pltpu.CompilerParams(vmem_limit_bytes=VMEM_LIMIT)


def _ffn_kernel(x_ref, pre_ref, post_ref, wg_ref, wu_ref, wd_ref, o_ref):
    n_sub = FFN_TILE // FFN_SUB

    def rows(t):
        return slice(t * FFN_SUB, (t + 1) * FFN_SUB)

    def pre(t):
        return _rms(x_ref[rows(t), :], pre_ref[...]).astype(BF16)

    def finish(t, y):
        o_ref[rows(t), :] = x_ref[rows(t), :] + 0.5 * _rms(y, post_ref[...])

    h = pre(0)
    y_prev = None
    for t in range(n_sub):
        y = None
        h_next = None
        for c, (start, size) in enumerate(FF_CHUNKS):
            g = jnp.dot(h, wg_ref[:, start:start + size], preferred_element_type=F32)
            u = jnp.dot(h, wu_ref[:, start:start + size], preferred_element_type=F32)
            a = (g * jax.nn.sigmoid(g) * u).astype(BF16)
            part = jnp.dot(a, wd_ref[start:start + size, :], preferred_element_type=F32)
            y = part if y is None else y + part
            if c == 0:
                if t + 1 < n_sub:
                    h_next = pre(t + 1)
                if t > 0:
                    finish(t - 1, y_prev)
        h, y_prev = h_next, y
    finish(n_sub - 1, y_prev)


def _ffn(x2d, pre_g, post_g, wg, wu, wd):
    m = x2d.shape[0]
    return pl.pallas_call(
        _ffn_kernel,
        grid=(m // FFN_TILE,),
        in_specs=[
            pl.BlockSpec((FFN_TILE, D_MODEL), lambda i: (i, 0)),
            _const_spec((1, D_MODEL)),
            _const_spec((1, D_MODEL)),
            _const_spec((D_MODEL, D_FF)),
            _const_spec((D_MODEL, D_FF)),
            _const_spec((D_FF, D_MODEL)),
        ],
        out_specs=pl.BlockSpec((FFN_TILE, D_MODEL), lambda i: (i, 0)),
        out_shape=jax.ShapeDtypeStruct((m, D_MODEL), F32),
        compiler_params=_params(),
    )(x2d, pre_g, post_g, wg, wu, wd)


def _inproj_kernel(x_ref, pre_ref, w_ref, rot_ref,
                   a0_ref, a1_ref, a2_ref, b_ref, slab_ref, tmp_ref):
    h = _rms(x_ref[0], pre_ref[...]).astype(BF16)
    half = ROT_DIM // 2
    packed = rot_ref[...]
    head_lane = lax.broadcasted_iota(jnp.int32, packed.shape, 1) % HEAD_DIM
    cos = jnp.where(head_lane < ROT_DIM, packed, 1.0)
    sin_hi = jnp.where(head_lane < half, 0.0,
                       jnp.where(head_lane < ROT_DIM, pltpu.roll(packed, LANES - half, 1), 0.0))
    sin_lo = jnp.where(head_lane < half, pltpu.roll(packed, LANES - (ROT_DIM + half), 1), 0.0)
    n_slabs = A_WIDTH // LANES
    n_rot_slabs = 2 * A_GROUP_WIDTH // LANES
    for g, (d, out_ref) in enumerate(zip(DILATIONS, (a0_ref, a1_ref, a2_ref))):
        res = jnp.dot(h, w_ref[:, g * A_WIDTH:(g + 1) * A_WIDTH], preferred_element_type=F32)
        n = TOKEN_TILE // d
        for c in range(n_slabs):
            xs = res[:, c * LANES:(c + 1) * LANES]
            if c < n_rot_slabs:
                xs = (xs * cos + pltpu.roll(xs, ROT_DIM // 2, 1) * sin_hi
                      + pltpu.roll(xs, LANES - ROT_DIM // 2, 1) * sin_lo)
            if d == 1:
                out_ref[0, 0, :, c * LANES:(c + 1) * LANES] = xs.astype(BF16)
            else:
                slab_ref[c] = xs
        if d == DEINTERLEAVE_STRIDE:
            for c in range(n_slabs):
                for r in range(d):
                    rows = slab_ref[c, pl.ds(r, n, stride=d), :]
                    out_ref[0, r, :, c * LANES:(c + 1) * LANES] = rows.astype(BF16)
        elif d > 1:
            st = DEINTERLEAVE_STRIDE
            quarter = TOKEN_TILE // st
            for c in range(n_slabs):
                for r0 in range(st):
                    tmp_ref[c, r0 * quarter:(r0 + 1) * quarter, :] = slab_ref[c, pl.ds(r0, quarter, stride=st), :]
                for r1 in range(d // st):
                    for r0 in range(st):
                        rows = tmp_ref[c, pl.ds(r0 * quarter + r1, n, stride=st), :]
                        out_ref[0, st * r1 + r0, :, c * LANES:(c + 1) * LANES] = rows.astype(BF16)
    res = jnp.dot(h, w_ref[:, 3 * A_WIDTH:], preferred_element_type=F32)
    b_ref[0] = res.astype(BF16)


def _inproj(x, pre_g, w_proj, rot_table):
    bsz, seq, _ = x.shape
    nblk = seq // TOKEN_TILE
    tab_spec = pl.BlockSpec((TOKEN_TILE, LANES), lambda b, i: (i, 0))
    out_shapes = [jax.ShapeDtypeStruct((bsz, d, seq // d, A_WIDTH), BF16) for d in DILATIONS]
    out_specs = [pl.BlockSpec((1, d, TOKEN_TILE // d, A_WIDTH), lambda b, i: (b, 0, i, 0)) for d in DILATIONS]
    out_shapes.append(jax.ShapeDtypeStruct((bsz, seq, 3 * B_WIDTH), BF16))
    out_specs.append(pl.BlockSpec((1, TOKEN_TILE, 3 * B_WIDTH), lambda b, i: (b, i, 0)))
    return pl.pallas_call(
        _inproj_kernel,
        grid=(bsz, nblk),
        in_specs=[
            pl.BlockSpec((1, TOKEN_TILE, D_MODEL), lambda b, i: (b, i, 0)),
            _const_spec((1, D_MODEL)),
            _const_spec((D_MODEL, 3 * A_WIDTH + 3 * B_WIDTH)),
            tab_spec,
        ],
        out_specs=out_specs,
        out_shape=out_shapes,
        scratch_shapes=[pltpu.VMEM((A_WIDTH // LANES, TOKEN_TILE, LANES), F32)] * 2,
        compiler_params=_params(),
    )(x, pre_g, w_proj, rot_table)


def _head_stack(q, n_heads):
    rows, width = q.shape
    lane_head = lax.broadcasted_iota(jnp.int32, (rows, width), 1) // HEAD_DIM
    zero = jnp.zeros_like(q)
    return jnp.concatenate([jnp.where(lane_head == hh, q, zero) for hh in range(n_heads)], axis=0)


def _softmax_parts(s):
    m = jnp.max(s, axis=-1, keepdims=True)
    p = jnp.exp(s - m)
    l = jnp.sum(p, axis=-1, keepdims=True)
    return p.astype(BF16), 1.0 / l, m + jnp.log(l)


def _band_bias():
    r = BAND_RADIUS
    q_off = np.arange(BAND_Q_TILE)[:, None]
    col = np.arange(BAND_Q_TILE + 2 * r)[None, :]
    band = (col >= q_off) & (col <= q_off + 2 * r)
    variants = []
    for v in range(4):
        ok = band
        if v & 1:
            ok = ok & (col >= r)
        if v & 2:
            ok = ok & (col < BAND_Q_TILE + r)
        variants.append(np.where(ok, 0.0, NEG).astype(np.float32))
    return jnp.asarray(np.stack(variants))


def _attn_a_kernel(q_ref, kp_ref, kc_ref, kn_ref, vp_ref, vc_ref, vn_ref, bias_ref, ol_ref, k_ext, v_ext,
                   *, q_block):
    i = pl.program_id(1)
    last = pl.num_programs(1) - 1
    r = BAND_RADIUS
    k_ext[0:r] = kp_ref[0]
    k_ext[r:r + q_block] = kc_ref[0]
    k_ext[r + q_block:] = kn_ref[0]
    v_ext[0:r] = vp_ref[0]
    v_ext[r:r + q_block] = vc_ref[0]
    v_ext[r + q_block:] = vn_ref[0]
    n_keys = BAND_Q_TILE + 2 * r
    n_tiles = q_block // BAND_Q_TILE

    def scores(j):
        q = _head_stack(q_ref[0, j * BAND_Q_TILE:(j + 1) * BAND_Q_TILE, :], A_GROUP_HEADS)
        k = k_ext[j * BAND_Q_TILE:j * BAND_Q_TILE + n_keys, :]
        return lax.dot_general(q, k, (((1,), (1,)), ((), ())), preferred_element_type=F32)

    s_next = scores(0)
    for j in range(n_tiles):
        variant = 0
        if j == 0:
            variant = variant + jnp.where(i == 0, 1, 0)
        if j == n_tiles - 1:
            variant = variant + jnp.where(i == last, 2, 0)
        v = v_ext[j * BAND_Q_TILE:j * BAND_Q_TILE + n_keys, :]
        s = s_next
        if j + 1 < n_tiles:
            s_next = scores(j + 1)
        sl = slice(j * BAND_Q_TILE, (j + 1) * BAND_Q_TILE)
        probs, scales = [], []
        for hh in range(A_GROUP_HEADS):
            p, inv_l, lse_h = _softmax_parts(s[hh * BAND_Q_TILE:(hh + 1) * BAND_Q_TILE] + bias_ref[variant])
            probs.append(p)
            scales.append(inv_l)
            lse_lanes = slice(A_GROUP_WIDTH + hh * LSE_LANES, A_GROUP_WIDTH + (hh + 1) * LSE_LANES)
            ol_ref[0, sl, lse_lanes] = jnp.broadcast_to(lse_h, (BAND_Q_TILE, LSE_LANES))
        o = jnp.dot(jnp.concatenate(probs, axis=0), v, preferred_element_type=F32)
        for hh in range(A_GROUP_HEADS):
            lanes = slice(hh * HEAD_DIM, (hh + 1) * HEAD_DIM)
            ol_ref[0, sl, lanes] = o[hh * BAND_Q_TILE:(hh + 1) * BAND_Q_TILE, lanes] * scales[hh]


def _attn_a(qkv, band_bias):
    n_seq, seq_len, _ = qkv.shape
    q_block = min(BAND_Q_BLOCK, seq_len)
    halo_per_block = q_block // BAND_RADIUS
    n_halo = seq_len // BAND_RADIUS

    def cur(col):
        return pl.BlockSpec((1, q_block, A_GROUP_WIDTH), lambda s, i: (s, i, col))

    def prev(col):
        return pl.BlockSpec((1, BAND_RADIUS, A_GROUP_WIDTH),
                            lambda s, i: (s, jnp.maximum(i * halo_per_block - 1, 0), col))

    def nxt(col):
        return pl.BlockSpec((1, BAND_RADIUS, A_GROUP_WIDTH),
                            lambda s, i: (s, jnp.minimum((i + 1) * halo_per_block, n_halo - 1), col))

    ext = pltpu.VMEM((q_block + 2 * BAND_RADIUS, A_GROUP_WIDTH), BF16)
    return pl.pallas_call(
        functools.partial(_attn_a_kernel, q_block=q_block),
        grid=(n_seq, seq_len // q_block),
        in_specs=[cur(0), prev(1), cur(1), nxt(1), prev(2), cur(2), nxt(2), _const_spec(band_bias.shape)],
        out_specs=pl.BlockSpec((1, q_block, OL_WIDTH), lambda s, i: (s, i, 0)),
        out_shape=jax.ShapeDtypeStruct((n_seq, seq_len, OL_WIDTH), F32),
        scratch_shapes=[ext, ext],
        compiler_params=_params(),
    )(qkv, qkv, qkv, qkv, qkv, qkv, qkv, band_bias)


def _attn_b_kernel(q_ref, kp_ref, kc_ref, kn_ref, vp_ref, vc_ref, vn_ref, bias_ref, o_ref, k_ext, v_ext):
    i = pl.program_id(1)
    last = pl.num_programs(1) - 1
    halo = (NA_ROWS // 2) * GRID_W
    step_tokens = B_ROWS_PER_STEP * GRID_W
    win = NA_ROWS * GRID_W
    k_ext[0:halo] = kp_ref[0]
    k_ext[halo:halo + step_tokens] = kc_ref[0]
    k_ext[halo + step_tokens:] = kn_ref[0]
    v_ext[0:halo] = vp_ref[0]
    v_ext[halo:halo + step_tokens] = vc_ref[0]
    v_ext[halo + step_tokens:] = vn_ref[0]
    lo = jnp.where(i == 0, NA_ROWS // 2, 0)
    hi = jnp.where(i == last, B_ROWS_PER_STEP - NA_ROWS // 2, B_ROWS_PER_STEP - 1)
    n_halves = B_HEADS // B_HALF_HEADS
    units = [(rr, half) for rr in range(B_ROWS_PER_STEP) for half in range(n_halves)]

    def window(rr):
        w_row = jnp.clip(rr, lo, hi)
        start = pl.multiple_of(w_row * GRID_W, GRID_W)
        delta = w_row - rr + (NA_ROWS // 2 - 1)
        return start, delta

    def scores(rr, half):
        start, _ = window(rr)
        lanes = slice(half * B_HALF_WIDTH, (half + 1) * B_HALF_WIDTH)
        q = _head_stack(q_ref[0, rr * GRID_W:(rr + 1) * GRID_W, lanes], B_HALF_HEADS)
        k = k_ext[pl.ds(start, win), lanes]
        return lax.dot_general(q, k, (((1,), (1,)), ((), ())), preferred_element_type=F32)

    s_next = scores(*units[0])
    for u, (rr, half) in enumerate(units):
        start, delta = window(rr)
        rows = slice(rr * GRID_W, (rr + 1) * GRID_W)
        lanes = slice(half * B_HALF_WIDTH, (half + 1) * B_HALF_WIDTH)
        s = s_next
        if u + 1 < len(units):
            s_next = scores(*units[u + 1])
        probs, scales = [], []
        for hh in range(B_HALF_HEADS):
            b_rows = pl.ds((half * B_HALF_HEADS + hh) * GRID_W, GRID_W)
            p, inv_l, _ = _softmax_parts(s[hh * GRID_W:(hh + 1) * GRID_W] + bias_ref[delta, b_rows, :])
            probs.append(p)
            scales.append(inv_l)
        v = v_ext[pl.ds(start, win), lanes]
        o = jnp.dot(jnp.concatenate(probs, axis=0), v, preferred_element_type=F32)
        for hh in range(B_HALF_HEADS):
            h_lanes = slice(hh * HEAD_DIM, (hh + 1) * HEAD_DIM)
            o_h = o[hh * GRID_W:(hh + 1) * GRID_W, h_lanes] * scales[hh]
            col = half * B_HALF_WIDTH + hh * HEAD_DIM
            o_ref[0, rows, col:col + HEAD_DIM] = o_h.astype(BF16)


def _attn_b(qkv, bias):
    bsz, seq, _ = qkv.shape
    step_tokens = B_ROWS_PER_STEP * GRID_W
    halo = (NA_ROWS // 2) * GRID_W
    per = step_tokens // halo
    n_halo = seq // halo

    def cur(col):
        return pl.BlockSpec((1, step_tokens, B_WIDTH), lambda b, i: (b, i, col))

    def prev(col):
        return pl.BlockSpec((1, halo, B_WIDTH), lambda b, i: (b, jnp.maximum(i * per - 1, 0), col))

    def nxt(col):
        return pl.BlockSpec((1, halo, B_WIDTH), lambda b, i: (b, jnp.minimum((i + 1) * per, n_halo - 1), col))

    ext = pltpu.VMEM((step_tokens + 2 * halo, B_WIDTH), BF16)
    return pl.pallas_call(
        _attn_b_kernel,
        grid=(bsz, seq // step_tokens),
        in_specs=[cur(0), prev(1), cur(1), nxt(1), prev(2), cur(2), nxt(2),
                  _const_spec(bias.shape)],
        out_specs=pl.BlockSpec((1, step_tokens, B_WIDTH), lambda b, i: (b, i, 0)),
        out_shape=jax.ShapeDtypeStruct((bsz, seq, B_WIDTH), BF16),
        scratch_shapes=[ext, ext],
        compiler_params=_params(),
    )(qkv, qkv, qkv, qkv, qkv, qkv, qkv, bias)


MIX_SUB = 256


def _mixout_kernel(x_ref, ol0_ref, ol1_ref, ol2_ref, ob_ref,
                   pre_ref, post_ref, wgate_ref, bgate_ref, wa_ref, wb_ref, wout_ref,
                   y_ref, nat_ref, tmp_ref):
    n_slabs = OL_WIDTH // LANES
    o_slabs = A_GROUP_WIDTH // LANES
    for k, (d, src) in enumerate(((DILATIONS[1], ol1_ref), (DILATIONS[2], ol2_ref))):
        n = TOKEN_TILE // d
        st = DEINTERLEAVE_STRIDE
        quarter = TOKEN_TILE // st
        for c in range(n_slabs):
            dst = k * n_slabs + c
            if d == st:
                for r in range(d):
                    nat_ref[dst, pl.ds(r, n, stride=d), :] = src[0, r, :, c * LANES:(c + 1) * LANES]
            else:
                for r1 in range(d // st):
                    for r0 in range(st):
                        tmp_ref[c, pl.ds(r0 * quarter + r1, n, stride=st), :] = \
                            src[0, st * r1 + r0, :, c * LANES:(c + 1) * LANES]
                for r0 in range(st):
                    nat_ref[dst, pl.ds(r0, quarter, stride=st), :] = tmp_ref[c, r0 * quarter:(r0 + 1) * quarter, :]

    def nat_o(k, sl):
        return jnp.concatenate([nat_ref[k * n_slabs + c, sl, :] for c in range(o_slabs)], axis=-1)

    def nat_lse(k, sl):
        return nat_ref[k * n_slabs + o_slabs, sl, :]

    def spread_heads(c):
        lane = lax.broadcasted_iota(jnp.int32, c.shape, 1)
        by1, by2, by3 = (pltpu.roll(c, s * LSE_LANES, 1) for s in (1, 2, 3))
        first, last = lane < LSE_LANES, lane >= 3 * LSE_LANES
        lo = jnp.where(first, c, jnp.where(last, by2, by1))
        hi = jnp.where(first, by2, jnp.where(last, c, by3))
        return jnp.concatenate([lo, hi], axis=-1)

    for t in range(TOKEN_TILE // MIX_SUB):
        sl = slice(t * MIX_SUB, (t + 1) * MIX_SUB)
        x = x_ref[0, sl, :]
        h = _rms(x, pre_ref[...]).astype(BF16)
        gates = jax.nn.sigmoid(jnp.dot(h, wgate_ref[...], preferred_element_type=F32) + bgate_ref[...])
        o0, l0 = ol0_ref[0, 0, sl, :A_GROUP_WIDTH], ol0_ref[0, 0, sl, A_GROUP_WIDTH:]
        o1, l1, o2, l2 = nat_o(0, sl), nat_lse(0, sl), nat_o(1, sl), nat_lse(1, sl)
        mx = jnp.maximum(jnp.maximum(l0, l1), l2)
        e0, e1, e2 = jnp.exp(l0 - mx), jnp.exp(l1 - mx), jnp.exp(l2 - mx)
        inv = 1.0 / (e0 + e1 + e2)
        oa = (spread_heads(e0 * inv) * o0 + spread_heads(e1 * inv) * o1 + spread_heads(e2 * inv) * o2)
        ya = jnp.dot(oa.astype(BF16), wa_ref[...], preferred_element_type=F32)
        yb = jnp.dot(ob_ref[0, sl, :], wb_ref[...], preferred_element_type=F32)
        mixed = (gates[:, :D_MODEL] * ya + gates[:, D_MODEL:] * yb).astype(BF16)
        m = jnp.dot(mixed, wout_ref[...], preferred_element_type=F32)
        y_ref[0, sl, :] = x + _rms(m, post_ref[...])


def _mixout(x, ols, ob, pre_g, post_g, w_gate, b_gate, w_a, w_b, w_out):
    bsz, seq, _ = x.shape
    nblk = seq // TOKEN_TILE
    tile_spec = pl.BlockSpec((1, TOKEN_TILE, D_MODEL), lambda b, i: (b, i, 0))
    in_specs = [tile_spec]
    operands = [x]
    for d, ol in zip(DILATIONS, ols):
        in_specs.append(pl.BlockSpec((1, d, TOKEN_TILE // d, OL_WIDTH), lambda b, i: (b, 0, i, 0)))
        operands.append(ol)
    in_specs.append(pl.BlockSpec((1, TOKEN_TILE, B_WIDTH), lambda b, i: (b, i, 0)))
    operands.append(ob)
    for w in (pre_g, post_g, w_gate, b_gate, w_a, w_b, w_out):
        in_specs.append(_const_spec(w.shape))
        operands.append(w)
    return pl.pallas_call(
        _mixout_kernel,
        grid=(bsz, nblk),
        in_specs=in_specs,
        out_specs=tile_spec,
        out_shape=jax.ShapeDtypeStruct((bsz, seq, D_MODEL), F32),
        scratch_shapes=[pltpu.VMEM((2 * OL_WIDTH // LANES, TOKEN_TILE, LANES), F32),
                        pltpu.VMEM((OL_WIDTH // LANES, TOKEN_TILE, LANES), F32)],
        compiler_params=_params(),
    )(*operands)


def _rotary_tables(seq):
    inv_freq = ROPE_THETA ** (-jnp.arange(0, ROT_DIM, 2, dtype=F32) / ROT_DIM)
    ang = jnp.arange(seq, dtype=F32)[:, None] * inv_freq[None, :]
    cos, sin = jnp.cos(ang), jnp.sin(ang)
    unused = jnp.ones((seq, HEAD_DIM - 2 * ROT_DIM), F32)
    per_head = jnp.concatenate([cos, cos, sin, -sin, unused], axis=1)
    return jnp.tile(per_head, (1, LANES // HEAD_DIM))


def _neighbourhood_bias(rpb):
    cj = np.arange(GRID_W)
    cs = np.clip(cj - NA_COLS // 2, 0, GRID_W - NA_COLS)
    col_valid = (cj[None, :] >= cs[:, None]) & (cj[None, :] < cs[:, None] + NA_COLS)
    dc_idx = np.clip(cj[None, :] - cj[:, None] + (NA_COLS - 1), 0, 2 * NA_COLS - 2)
    n_dc = 2 * NA_COLS - 1
    onehot = (dc_idx[None] == np.arange(n_dc)[:, None, None]).astype(np.float32)
    base = jnp.einsum("hrd,dqk->hrqk", rpb.astype(F32), jnp.asarray(onehot), precision=lax.Precision.HIGHEST)
    base = jnp.where(col_valid[None, None], base, NEG)
    tables = []
    for first in range(NA_ROWS):
        t = base[:, first:first + NA_ROWS].transpose(0, 2, 1, 3)
        tables.append(t.reshape(B_HEADS * GRID_W, NA_ROWS * GRID_W))
    return jnp.stack(tables, axis=0)


def _split_w_in(w_in):
    scale = HEAD_DIM ** -0.5
    o = 0
    qa = w_in[:, o:o + A_WIDTH] * scale; o += A_WIDTH
    ka = w_in[:, o:o + A_WIDTH]; o += A_WIDTH
    va = w_in[:, o:o + A_WIDTH]; o += A_WIDTH
    qb = w_in[:, o:o + B_WIDTH] * scale; o += B_WIDTH
    kb = w_in[:, o:o + B_WIDTH]; o += B_WIDTH
    vb = w_in[:, o:o + B_WIDTH]; o += B_WIDTH
    w_gate = w_in[:, o:]
    cols = []
    for g in range(len(DILATIONS)):
        sl = slice(g * A_GROUP_WIDTH, (g + 1) * A_GROUP_WIDTH)
        cols += [qa[:, sl], ka[:, sl], va[:, sl]]
    cols += [qb, kb, vb]
    return jnp.concatenate(cols, axis=1).astype(BF16), w_gate.astype(BF16)


def _encoder(x, p, tables):
    bsz, seq, _ = x.shape
    row = lambda v: v.reshape(1, -1)
    x1 = _ffn(x.reshape(bsz * seq, D_MODEL), row(p["ffn1_pre_g"]), row(p["ffn1_post_g"]),
              p["ffn1_wg"], p["ffn1_wu"], p["ffn1_wd"]).reshape(bsz, seq, D_MODEL)
    a0, a1, a2, qkvb = _inproj(x1, row(p["mix_pre_g"]), p["w_proj"], tables)
    ols = []
    band_bias = _band_bias()
    for d, qkv in zip(DILATIONS, (a0, a1, a2)):
        ol = _attn_a(qkv.reshape(bsz * d, seq // d, A_WIDTH), band_bias)
        ols.append(ol.reshape(bsz, d, seq // d, OL_WIDTH))
    ob = _attn_b(qkvb, p["bias_b"])
    x2 = _mixout(x1, ols, ob, row(p["mix_pre_g"]), row(p["mix_post_g"]), p["w_gate"], row(p["b_gate"]),
                 p["w_a"], p["w_b"], p["w_out"])
    x3 = _ffn(x2.reshape(bsz * seq, D_MODEL), row(p["ffn2_pre_g"]), row(p["ffn2_post_g"]),
              p["ffn2_wg"], p["ffn2_wu"], p["ffn2_wd"])
    return x3.reshape(bsz, seq, D_MODEL)


def kernel(x_prompt, x_sample, ffn1_pre_g, ffn1_post_g, ffn1_w_gate, ffn1_w_up, ffn1_w_down, mix_pre_g, mix_post_g,
           w_in, b_gate, rpb, w_branch_a, w_branch_b, w_out, ffn2_pre_g, ffn2_post_g, ffn2_w_gate, ffn2_w_up,
           ffn2_w_down):
    w_proj, w_gate = _split_w_in(w_in[0])
    p = {
        "ffn1_pre_g": ffn1_pre_g[0], "ffn1_post_g": ffn1_post_g[0],
        "ffn1_wg": ffn1_w_gate[0].astype(BF16), "ffn1_wu": ffn1_w_up[0].astype(BF16),
        "ffn1_wd": ffn1_w_down[0].astype(BF16),
        "mix_pre_g": mix_pre_g[0], "mix_post_g": mix_post_g[0],
        "w_proj": w_proj, "w_gate": w_gate, "b_gate": b_gate[0],
        "bias_b": _neighbourhood_bias(rpb[0]),
        "w_a": w_branch_a[0].astype(BF16), "w_b": w_branch_b[0].astype(BF16), "w_out": w_out[0].astype(BF16),
        "ffn2_pre_g": ffn2_pre_g[0], "ffn2_post_g": ffn2_post_g[0],
        "ffn2_wg": ffn2_w_gate[0].astype(BF16), "ffn2_wu": ffn2_w_up[0].astype(BF16),
        "ffn2_wd": ffn2_w_down[0].astype(BF16),
    }
    tables = _rotary_tables(max(x_prompt.shape[1], x_sample.shape[1]))
    return _encoder(x_prompt, p, tables), _encoder(x_sample, p, tables)
```

```python
import functools

import jax
import jax.numpy as jnp
import numpy as np
from jax import lax
from jax.experimental import pallas as pl
from jax.experimental.pallas import tpu as pltpu

D_MODEL = 1024
D_FF = 2816
HEAD_DIM = 64
DILATIONS = (1, 4, 16)
BAND_RADIUS = 64
A_GROUP_HEADS = 4
A_GROUP_WIDTH = A_GROUP_HEADS * HEAD_DIM
A_WIDTH = 3 * A_GROUP_WIDTH
B_HEADS = 8
B_WIDTH = B_HEADS * HEAD_DIM
GRID_W = 64
NA_ROWS = 8
NA_COLS = 16
ROPE_THETA = 500000.0
ROT_DIM = HEAD_DIM // 4
RMS_EPS = 1e-6
NEG = -1e30

LANES = 128
VMEM_LIMIT = 56 * 1024 * 1024

TOKEN_TILE = 1024
DEINTERLEAVE_STRIDE = 4
FFN_TILE = 1024
FFN_SUB = 256
FF_CHUNKS = ((0, 1024), (1024, 1024), (2048, 768))
BAND_Q_BLOCK = 4096
BAND_Q_TILE = 128
LSE_LANES = LANES // A_GROUP_HEADS
OL_WIDTH = A_GROUP_WIDTH + LANES
B_ROWS_PER_STEP = 32
B_HALF_HEADS = 4
B_HALF_WIDTH = B_HALF_HEADS * HEAD_DIM

BF16 = jnp.bfloat16
F32 = jnp.float32


def _rms(x, g):
    return x * lax.rsqrt(jnp.mean(x * x, axis=-1, keepdims=True) + RMS_EPS) * g


def _const_spec(shape):
    zeros = (0,) * len(shape)
    return pl.BlockSpec(shape, lambda *_: zeros, pipeline_mode=pl.Buffered(1))


def _params():
    return pltpu.CompilerParams(vmem_limit_bytes=VMEM_LIMIT)


def _ffn_kernel(x_ref, pre_ref, post_ref, wg_ref, wu_ref, wd_ref, o_ref):
    n_sub = FFN_TILE // FFN_SUB

    def rows(t):
        return slice(t * FFN_SUB, (t + 1) * FFN_SUB)

    def pre(t):
        return _rms(x_ref[rows(t), :], pre_ref[...]).astype(BF16)

    def finish(t, y):
        o_ref[rows(t), :] = x_ref[rows(t), :] + 0.5 * _rms(y, post_ref[...])

    h = pre(0)
    y_prev = None
    for t in range(n_sub):
        y = None
        h_next = None
        for c, (start, size) in enumerate(FF_CHUNKS):
            g = jnp.dot(h, wg_ref[:, start:start + size], preferred_element_type=F32)
            u = jnp.dot(h, wu_ref[:, start:start + size], preferred_element_type=F32)
            a = (g * jax.nn.sigmoid(g) * u).astype(BF16)
            part = jnp.dot(a, wd_ref[start:start + size, :], preferred_element_type=F32)
            y = part if y is None else y + part
            if c == 0:
                if t + 1 < n_sub:
                    h_next = pre(t + 1)
                if t > 0:
                    finish(t - 1, y_prev)
        h, y_prev = h_next, y
    finish(n_sub - 1, y_prev)


def _ffn(x2d, pre_g, post_g, wg, wu, wd):
    m = x2d.shape[0]
    return pl.pallas_call(
        _ffn_kernel,
        grid=(m // FFN_TILE,),
        in_specs=[
            pl.BlockSpec((FFN_TILE, D_MODEL), lambda i: (i, 0)),
            _const_spec((1, D_MODEL)),
            _const_spec((1, D_MODEL)),
            _const_spec((D_MODEL, D_FF)),
            _const_spec((D_MODEL, D_FF)),
            _const_spec((D_FF, D_MODEL)),
        ],
        out_specs=pl.BlockSpec((FFN_TILE, D_MODEL), lambda i: (i, 0)),
        out_shape=jax.ShapeDtypeStruct((m, D_MODEL), F32),
        compiler_params=_params(),
    )(x2d, pre_g, post_g, wg, wu, wd)


def _inproj_kernel(x_ref, pre_ref, w_ref, rot_ref,
                   a0_ref, a1_ref, a2_ref, b_ref, slab_ref, tmp_ref):
    h = _rms(x_ref[0], pre_ref[...]).astype(BF16)
    half = ROT_DIM // 2
    packed = rot_ref[...]
    head_lane = lax.broadcasted_iota(jnp.int32, packed.shape, 1) % HEAD_DIM
    cos = jnp.where(head_lane < ROT_DIM, packed, 1.0)
    sin_hi = jnp.where(head_lane < half, 0.0,
                       jnp.where(head_lane < ROT_DIM, pltpu.roll(packed, LANES - half, 1), 0.0))
    sin_lo = jnp.where(head_lane < half, pltpu.roll(packed, LANES - (ROT_DIM + half), 1), 0.0)
    n_slabs = A_WIDTH // LANES
    n_rot_slabs = 2 * A_GROUP_WIDTH // LANES
    for g, (d, out_ref) in enumerate(zip(DILATIONS, (a0_ref, a1_ref, a2_ref))):
        res = jnp.dot(h, w_ref[:, g * A_WIDTH:(g + 1) * A_WIDTH], preferred_element_type=F32)
        n = TOKEN_TILE // d
        for c in range(n_slabs):
            xs = res[:, c * LANES:(c + 1) * LANES]
            if c < n_rot_slabs:
                xs = (xs * cos + pltpu.roll(xs, ROT_DIM // 2, 1) * sin_hi
                      + pltpu.roll(xs, LANES - ROT_DIM // 2, 1) * sin_lo)
            if d == 1:
                out_ref[0, 0, :, c * LANES:(c + 1) * LANES] = xs.astype(BF16)
            else:
                slab_ref[c] = xs
        if d == DEINTERLEAVE_STRIDE:
            for c in range(n_slabs):
                for r in range(d):
                    rows = slab_ref[c, pl.ds(r, n, stride=d), :]
                    out_ref[0, r, :, c * LANES:(c + 1) * LANES] = rows.astype(BF16)
        elif d > 1:
            st = DEINTERLEAVE_STRIDE
            quarter = TOKEN_TILE // st
            for c in range(n_slabs):
                for r0 in range(st):
                    tmp_ref[c, r0 * quarter:(r0 + 1) * quarter, :] = slab_ref[c, pl.ds(r0, quarter, stride=st), :]
                for r1 in range(d // st):
                    for r0 in range(st):
                        rows = tmp_ref[c, pl.ds(r0 * quarter + r1, n, stride=st), :]
                        out_ref[0, st * r1 + r0, :, c * LANES:(c + 1) * LANES] = rows.astype(BF16)
    res = jnp.dot(h, w_ref[:, 3 * A_WIDTH:], preferred_element_type=F32)
    b_ref[0] = res.astype(BF16)


def _inproj(x, pre_g, w_proj, rot_table):
    bsz, seq, _ = x.shape
    nblk = seq // TOKEN_TILE
    tab_spec = pl.BlockSpec((TOKEN_TILE, LANES), lambda b, i: (i, 0))
    out_shapes = [jax.ShapeDtypeStruct((bsz, d, seq // d, A_WIDTH), BF16) for d in DILATIONS]
    out_specs = [pl.BlockSpec((1, d, TOKEN_TILE // d, A_WIDTH), lambda b, i: (b, 0, i, 0)) for d in DILATIONS]
    out_shapes.append(jax.ShapeDtypeStruct((bsz, seq, 3 * B_WIDTH), BF16))
    out_specs.append(pl.BlockSpec((1, TOKEN_TILE, 3 * B_WIDTH), lambda b, i: (b, i, 0)))
    return pl.pallas_call(
        _inproj_kernel,
        grid=(bsz, nblk),
        in_specs=[
            pl.BlockSpec((1, TOKEN_TILE, D_MODEL), lambda b, i: (b, i, 0)),
            _const_spec((1, D_MODEL)),
            _const_spec((D_MODEL, 3 * A_WIDTH + 3 * B_WIDTH)),
            tab_spec,
        ],
        out_specs=out_specs,
        out_shape=out_shapes,
        scratch_shapes=[pltpu.VMEM((A_WIDTH // LANES, TOKEN_TILE, LANES), F32)] * 2,
        compiler_params=_params(),
    )(x, pre_g, w_proj, rot_table)


def _head_stack(q, n_heads):
    rows, width = q.shape
    lane_head = lax.broadcasted_iota(jnp.int32, (rows, width), 1) // HEAD_DIM
    zero = jnp.zeros_like(q)
    return jnp.concatenate([jnp.where(lane_head == hh, q, zero) for hh in range(n_heads)], axis=0)


def _softmax_parts(s):
    m = jnp.max(s, axis=-1, keepdims=True)
    p = jnp.exp(s - m)
    l = jnp.sum(p, axis=-1, keepdims=True)
    return p.astype(BF16), 1.0 / l, m + jnp.log(l)


def _band_bias():
    r = BAND_RADIUS
    q_off = np.arange(BAND_Q_TILE)[:, None]
    col = np.arange(BAND_Q_TILE + 2 * r)[None, :]
    band = (col >= q_off) & (col <= q_off + 2 * r)
    variants = []
    for v in range(4):
        ok = band
        if v & 1:
            ok = ok & (col >= r)
        if v & 2:
            ok = ok & (col < BAND_Q_TILE + r)
        variants.append(np.where(ok, 0.0, NEG).astype(np.float32))
    return jnp.asarray(np.stack(variants))


def _attn_a_kernel(q_ref, kp_ref, kc_ref, kn_ref, vp_ref, vc_ref, vn_ref, bias_ref, ol_ref, k_ext, v_ext,
                   *, q_block, n_seqs):
    i = pl.program_id(1)
    last = pl.num_programs(1) - 1
    r = BAND_RADIUS
    for g in range(n_seqs):
        k_ext[g, 0:r] = kp_ref[g]
        k_ext[g, r:r + q_block] = kc_ref[g]
        k_ext[g, r + q_block:] = kn_ref[g]
        v_ext[g, 0:r] = vp_ref[g]
        v_ext[g, r:r + q_block] = vc_ref[g]
        v_ext[g, r + q_block:] = vn_ref[g]
    n_keys = BAND_Q_TILE + 2 * r
    n_tiles = q_block // BAND_Q_TILE
    units = [(g, j) for g in range(n_seqs) for j in range(n_tiles)]

    def scores(g, j):
        q = _head_stack(q_ref[g, j * BAND_Q_TILE:(j + 1) * BAND_Q_TILE, :], A_GROUP_HEADS)
        k = k_ext[g, j * BAND_Q_TILE:j * BAND_Q_TILE + n_keys, :]
        return lax.dot_general(q, k, (((1,), (1,)), ((), ())), preferred_element_type=F32)

    s_next = scores(*units[0])
    for u, (g, j) in enumerate(units):
        variant = 0
        if j == 0:
            variant = variant + jnp.where(i == 0, 1, 0)
        if j == n_tiles - 1:
            variant = variant + jnp.where(i == last, 2, 0)
        v = v_ext[g, j * BAND_Q_TILE:j * BAND_Q_TILE + n_keys, :]
        s = s_next
        if u + 1 < len(units):
            s_next = scores(*units[u + 1])
        sl = slice(j * BAND_Q_TILE, (j + 1) * BAND_Q_TILE)
        probs, scales = [], []
        for hh in range(A_GROUP_HEADS):
            p, inv_l, lse_h = _softmax_parts(s[hh * BAND_Q_TILE:(hh + 1) * BAND_Q_TILE] + bias_ref[variant])
            probs.append(p)
            scales.append(inv_l)
            lse_lanes = slice(A_GROUP_WIDTH + hh * LSE_LANES, A_GROUP_WIDTH + (hh + 1) * LSE_LANES)
            ol_ref[g, sl, lse_lanes] = jnp.broadcast_to(lse_h, (BAND_Q_TILE, LSE_LANES))
        o = jnp.dot(jnp.concatenate(probs, axis=0), v, preferred_element_type=F32)
        for hh in range(A_GROUP_HEADS):
            lanes = slice(hh * HEAD_DIM, (hh + 1) * HEAD_DIM)
            ol_ref[g, sl, lanes] = o[hh * BAND_Q_TILE:(hh + 1) * BAND_Q_TILE, lanes] * scales[hh]


def _attn_a(qkv, band_bias):
    n_seq, seq_len, _ = qkv.shape
    q_block = min(BAND_Q_BLOCK, seq_len)
    n_seqs = min(n_seq, BAND_Q_BLOCK // q_block)
    assert n_seq % n_seqs == 0 and seq_len % q_block == 0
    halo_per_block = q_block // BAND_RADIUS
    n_halo = seq_len // BAND_RADIUS

    def cur(col):
        return pl.BlockSpec((n_seqs, q_block, A_GROUP_WIDTH), lambda s, i: (s, i, col))

    def prev(col):
        return pl.BlockSpec((n_seqs, BAND_RADIUS, A_GROUP_WIDTH),
                            lambda s, i: (s, jnp.maximum(i * halo_per_block - 1, 0), col))

    def nxt(col):
        return pl.BlockSpec((n_seqs, BAND_RADIUS, A_GROUP_WIDTH),
                            lambda s, i: (s, jnp.minimum((i + 1) * halo_per_block, n_halo - 1), col))

    ext = pltpu.VMEM((n_seqs, q_block + 2 * BAND_RADIUS, A_GROUP_WIDTH), BF16)
    return pl.pallas_call(
        functools.partial(_attn_a_kernel, q_block=q_block, n_seqs=n_seqs),
        grid=(n_seq // n_seqs, seq_len // q_block),
        in_specs=[cur(0), prev(1), cur(1), nxt(1), prev(2), cur(2), nxt(2), _const_spec(band_bias.shape)],
        out_specs=pl.BlockSpec((n_seqs, q_block, OL_WIDTH), lambda s, i: (s, i, 0)),
        out_shape=jax.ShapeDtypeStruct((n_seq, seq_len, OL_WIDTH), F32),
        scratch_shapes=[ext, ext],
        compiler_params=_params(),
    )(qkv, qkv, qkv, qkv, qkv, qkv, qkv, band_bias)


def _attn_b_kernel(q_ref, kp_ref, kc_ref, kn_ref, vp_ref, vc_ref, vn_ref, bias_ref, o_ref, k_ext, v_ext):
    i = pl.program_id(1)
    last = pl.num_programs(1) - 1
    halo = (NA_ROWS // 2) * GRID_W
    step_tokens = B_ROWS_PER_STEP * GRID_W
    win = NA_ROWS * GRID_W
    k_ext[0:halo] = kp_ref[0]
    k_ext[halo:halo + step_tokens] = kc_ref[0]
    k_ext[halo + step_tokens:] = kn_ref[0]
    v_ext[0:halo] = vp_ref[0]
    v_ext[halo:halo + step_tokens] = vc_ref[0]
    v_ext[halo + step_tokens:] = vn_ref[0]
    lo = jnp.where(i == 0, NA_ROWS // 2, 0)
    hi = jnp.where(i == last, B_ROWS_PER_STEP - NA_ROWS // 2, B_ROWS_PER_STEP - 1)
    n_halves = B_HEADS // B_HALF_HEADS
    units = [(rr, half) for rr in range(B_ROWS_PER_STEP) for half in range(n_halves)]

    def window(rr):
        w_row = jnp.clip(rr, lo, hi)
        start = pl.multiple_of(w_row * GRID_W, GRID_W)
        delta = w_row - rr + (NA_ROWS // 2 - 1)
        return start, delta

    def scores(rr, half):
        start, _ = window(rr)
        lanes = slice(half * B_HALF_WIDTH, (half + 1) * B_HALF_WIDTH)
        q = _head_stack(q_ref[0, rr * GRID_W:(rr + 1) * GRID_W, lanes], B_HALF_HEADS)
        k = k_ext[pl.ds(start, win), lanes]
        return lax.dot_general(q, k, (((1,), (1,)), ((), ())), preferred_element_type=F32)

    s_next = scores(*units[0])
    for u, (rr, half) in enumerate(units):
        start, delta = window(rr)
        rows = slice(rr * GRID_W, (rr + 1) * GRID_W)
        lanes = slice(half * B_HALF_WIDTH, (half + 1) * B_HALF_WIDTH)
        s = s_next
        if u + 1 < len(units):
            s_next = scores(*units[u + 1])
        probs, scales = [], []
        for hh in range(B_HALF_HEADS):
            b_rows = pl.ds((half * B_HALF_HEADS + hh) * GRID_W, GRID_W)
            p, inv_l, _ = _softmax_parts(s[hh * GRID_W:(hh + 1) * GRID_W] + bias_ref[delta, b_rows, :])
            probs.append(p)
            scales.append(inv_l)
        v = v_ext[pl.ds(start, win), lanes]
        o = jnp.dot(jnp.concatenate(probs, axis=0), v, preferred_element_type=F32)
        for hh in range(B_HALF_HEADS):
            h_lanes = slice(hh * HEAD_DIM, (hh + 1) * HEAD_DIM)
            o_h = o[hh * GRID_W:(hh + 1) * GRID_W, h_lanes] * scales[hh]
            col = half * B_HALF_WIDTH + hh * HEAD_DIM
            o_ref[0, rows, col:col + HEAD_DIM] = o_h.astype(BF16)


def _attn_b(qkv, bias):
    bsz, seq, _ = qkv.shape
    step_tokens = B_ROWS_PER_STEP * GRID_W
    halo = (NA_ROWS // 2) * GRID_W
    per = step_tokens // halo
    n_halo = seq // halo

    def cur(col):
        return pl.BlockSpec((1, step_tokens, B_WIDTH), lambda b, i: (b, i, col))

    def prev(col):
        return pl.BlockSpec((1, halo, B_WIDTH), lambda b, i: (b, jnp.maximum(i * per - 1, 0), col))

    def nxt(col):
        return pl.BlockSpec((1, halo, B_WIDTH), lambda b, i: (b, jnp.minimum((i + 1) * per, n_halo - 1), col))

    ext = pltpu.VMEM((step_tokens + 2 * halo, B_WIDTH), BF16)
    return pl.pallas_call(
        _attn_b_kernel,
        grid=(bsz, seq // step_tokens),
        in_specs=[cur(0), prev(1), cur(1), nxt(1), prev(2), cur(2), nxt(2),
                  _const_spec(bias.shape)],
        out_specs=pl.BlockSpec((1, step_tokens, B_WIDTH), lambda b, i: (b, i, 0)),
        out_shape=jax.ShapeDtypeStruct((bsz, seq, B_WIDTH), BF16),
        scratch_shapes=[ext, ext],
        compiler_params=_params(),
    )(qkv, qkv, qkv, qkv, qkv, qkv, qkv, bias)


MIX_SUB = 256


def _mixout_kernel(x_ref, ol0_ref, ol1_ref, ol2_ref, ob_ref,
                   pre_ref, post_ref, wgate_ref, bgate_ref, wa_ref, wb_ref, wout_ref,
                   y_ref, nat_ref, tmp_ref):
    n_slabs = OL_WIDTH // LANES
    o_slabs = A_GROUP_WIDTH // LANES
    for k, (d, src) in enumerate(((DILATIONS[1], ol1_ref), (DILATIONS[2], ol2_ref))):
        n = TOKEN_TILE // d
        st = DEINTERLEAVE_STRIDE
        quarter = TOKEN_TILE // st
        for c in range(n_slabs):
            dst = k * n_slabs + c
            if d == st:
                for r in range(d):
                    nat_ref[dst, pl.ds(r, n, stride=d), :] = src[0, r, :, c * LANES:(c + 1) * LANES]
            else:
                for r1 in range(d // st):
                    for r0 in range(st):
                        tmp_ref[c, pl.ds(r0 * quarter + r1, n, stride=st), :] = \
                            src[0, st * r1 + r0, :, c * LANES:(c + 1) * LANES]
                for r0 in range(st):
                    nat_ref[dst, pl.ds(r0, quarter, stride=st), :] = tmp_ref[c, r0 * quarter:(r0 + 1) * quarter, :]

    def nat_o(k, sl):
        return jnp.concatenate([nat_ref[k * n_slabs + c, sl, :] for c in range(o_slabs)], axis=-1)

    def nat_lse(k, sl):
        return nat_ref[k * n_slabs + o_slabs, sl, :]

    def spread_heads(c):
        lane = lax.broadcasted_iota(jnp.int32, c.shape, 1)
        by1, by2, by3 = (pltpu.roll(c, s * LSE_LANES, 1) for s in (1, 2, 3))
        first, last = lane < LSE_LANES, lane >= 3 * LSE_LANES
        lo = jnp.where(first, c, jnp.where(last, by2, by1))
        hi = jnp.where(first, by2, jnp.where(last, c, by3))
        return jnp.concatenate([lo, hi], axis=-1)

    for t in range(TOKEN_TILE // MIX_SUB):
        sl = slice(t * MIX_SUB, (t + 1) * MIX_SUB)
        x = x_ref[0, sl, :]
        h = _rms(x, pre_ref[...]).astype(BF16)
        gates = jax.nn.sigmoid(jnp.dot(h, wgate_ref[...], preferred_element_type=F32) + bgate_ref[...])
        o0, l0 = ol0_ref[0, 0, sl, :A_GROUP_WIDTH], ol0_ref[0, 0, sl, A_GROUP_WIDTH:]
        o1, l1, o2, l2 = nat_o(0, sl), nat_lse(0, sl), nat_o(1, sl), nat_lse(1, sl)
        mx = jnp.maximum(jnp.maximum(l0, l1), l2)
        e0, e1, e2 = jnp.exp(l0 - mx), jnp.exp(l1 - mx), jnp.exp(l2 - mx)
        inv = 1.0 / (e0 + e1 + e2)
        oa = (spread_heads(e0 * inv) * o0 + spread_heads(e1 * inv) * o1 + spread_heads(e2 * inv) * o2)
        ya = jnp.dot(oa.astype(BF16), wa_ref[...], preferred_element_type=F32)
        yb = jnp.dot(ob_ref[0, sl, :], wb_ref[...], preferred_element_type=F32)
        mixed = (gates[:, :D_MODEL] * ya + gates[:, D_MODEL:] * yb).astype(BF16)
        m = jnp.dot(mixed, wout_ref[...], preferred_element_type=F32)
        y_ref[0, sl, :] = x + _rms(m, post_ref[...])


def _mixout(x, ols, ob, pre_g, post_g, w_gate, b_gate, w_a, w_b, w_out):
    bsz, seq, _ = x.shape
    nblk = seq // TOKEN_TILE
    tile_spec = pl.BlockSpec((1, TOKEN_TILE, D_MODEL), lambda b, i: (b, i, 0))
    in_specs = [tile_spec]
    operands = [x]
    for d, ol in zip(DILATIONS, ols):
        in_specs.append(pl.BlockSpec((1, d, TOKEN_TILE // d, OL_WIDTH), lambda b, i: (b, 0, i, 0)))
        operands.append(ol)
    in_specs.append(pl.BlockSpec((1, TOKEN_TILE, B_WIDTH), lambda b, i: (b, i, 0)))
    operands.append(ob)
    for w in (pre_g, post_g, w_gate, b_gate, w_a, w_b, w_out):
        in_specs.append(_const_spec(w.shape))
        operands.append(w)
    return pl.pallas_call(
        _mixout_kernel,
        grid=(bsz, nblk),
        in_specs=in_specs,
        out_specs=tile_spec,
        out_shape=jax.ShapeDtypeStruct((bsz, seq, D_MODEL), F32),
        scratch_shapes=[pltpu.VMEM((2 * OL_WIDTH // LANES, TOKEN_TILE, LANES), F32),
                        pltpu.VMEM((OL_WIDTH // LANES, TOKEN_TILE, LANES), F32)],
        compiler_params=_params(),
    )(*operands)


def _rotary_tables(seq):
    inv_freq = ROPE_THETA ** (-jnp.arange(0, ROT_DIM, 2, dtype=F32) / ROT_DIM)
    ang = jnp.arange(seq, dtype=F32)[:, None] * inv_freq[None, :]
    cos, sin = jnp.cos(ang), jnp.sin(ang)
    unused = jnp.ones((seq, HEAD_DIM - 2 * ROT_DIM), F32)
    per_head = jnp.concatenate([cos, cos, sin, -sin, unused], axis=1)
    return jnp.tile(per_head, (1, LANES // HEAD_DIM))


def _neighbourhood_bias(rpb):
    cj = np.arange(GRID_W)
    cs = np.clip(cj - NA_COLS // 2, 0, GRID_W - NA_COLS)
    col_valid = (cj[None, :] >= cs[:, None]) & (cj[None, :] < cs[:, None] + NA_COLS)
    dc_idx = np.clip(cj[None, :] - cj[:, None] + (NA_COLS - 1), 0, 2 * NA_COLS - 2)
    n_dc = 2 * NA_COLS - 1
    onehot = (dc_idx[None] == np.arange(n_dc)[:, None, None]).astype(np.float32)
    base = jnp.einsum("hrd,dqk->hrqk", rpb.astype(F32), jnp.asarray(onehot), precision=lax.Precision.HIGHEST)
    base = jnp.where(col_valid[None, None], base, NEG)
    tables = []
    for first in range(NA_ROWS):
        t = base[:, first:first + NA_ROWS].transpose(0, 2, 1, 3)
        tables.append(t.reshape(B_HEADS * GRID_W, NA_ROWS * GRID_W))
    return jnp.stack(tables, axis=0)


def _split_w_in(w_in):
    scale = HEAD_DIM ** -0.5
    o = 0
    qa = w_in[:, o:o + A_WIDTH] * scale; o += A_WIDTH
    ka = w_in[:, o:o + A_WIDTH]; o += A_WIDTH
    va = w_in[:, o:o + A_WIDTH]; o += A_WIDTH
    qb = w_in[:, o:o + B_WIDTH] * scale; o += B_WIDTH
    kb = w_in[:, o:o + B_WIDTH]; o += B_WIDTH
    vb = w_in[:, o:o + B_WIDTH]; o += B_WIDTH
    w_gate = w_in[:, o:]
    cols = []
    for g in range(len(DILATIONS)):
        sl = slice(g * A_GROUP_WIDTH, (g + 1) * A_GROUP_WIDTH)
        cols += [qa[:, sl], ka[:, sl], va[:, sl]]
    cols += [qb, kb, vb]
    return jnp.concatenate(cols, axis=1).astype(BF16), w_gate.astype(BF16)


def _encoder(x, p, tables):
    bsz, seq, _ = x.shape
    row = lambda v: v.reshape(1, -1)
    x1 = _ffn(x.reshape(bsz * seq, D_MODEL), row(p["ffn1_pre_g"]), row(p["ffn1_post_g"]),
              p["ffn1_wg"], p["ffn1_wu"], p["ffn1_wd"]).reshape(bsz, seq, D_MODEL)
    a0, a1, a2, qkvb = _inproj(x1, row(p["mix_pre_g"]), p["w_proj"], tables)
    ols = []
    band_bias = _band_bias()
    for d, qkv in zip(DILATIONS, (a0, a1, a2)):
        ol = _attn_a(qkv.reshape(bsz * d, seq // d, A_WIDTH), band_bias)
        ols.append(ol.reshape(bsz, d, seq // d, OL_WIDTH))
    ob = _attn_b(qkvb, p["bias_b"])
    x2 = _mixout(x1, ols, ob, row(p["mix_pre_g"]), row(p["mix_post_g"]), p["w_gate"], row(p["b_gate"]),
                 p["w_a"], p["w_b"], p["w_out"])
    x3 = _ffn(x2.reshape(bsz * seq, D_MODEL), row(p["ffn2_pre_g"]), row(p["ffn2_post_g"]),
              p["ffn2_wg"], p["ffn2_wu"], p["ffn2_wd"])
    return x3.reshape(bsz, seq, D_MODEL)


def kernel(x_prompt, x_sample, ffn1_pre_g, ffn1_post_g, ffn1_w_gate, ffn1_w_up, ffn1_w_down, mix_pre_g, mix_post_g,
           w_in, b_gate, rpb, w_branch_a, w_branch_b, w_out, ffn2_pre_g, ffn2_post_g, ffn2_w_gate, ffn2_w_up,
           ffn2_w_down):
    w_proj, w_gate = _split_w_in(w_in[0])
    p = {
        "ffn1_pre_g": ffn1_pre_g[0], "ffn1_post_g": ffn1_post_g[0],
        "ffn1_wg": ffn1_w_gate[0].astype(BF16), "ffn1_wu": ffn1_w_up[0].astype(BF16),
        "ffn1_wd": ffn1_w_down[0].astype(BF16),
        "mix_pre_g": mix_pre_g[0], "mix_post_g": mix_post_g[0],
        "w_proj": w_proj, "w_gate": w_gate, "b_gate": b_gate[0],
        "bias_b": _neighbourhood_bias(rpb[0]),
        "w_a": w_branch_a[0].astype(BF16), "w_b": w_branch_b[0].astype(BF16), "w_out": w_out[0].astype(BF16),
        "ffn2_pre_g": ffn2_pre_g[0], "ffn2_post_g": ffn2_post_g[0],
        "ffn2_wg": ffn2_w_gate[0].astype(BF16), "ffn2_wu": ffn2_w_up[0].astype(BF16),
        "ffn2_wd": ffn2_w_down[0].astype(BF16),
    }
    tables = _rotary_tables(max(x_prompt.shape[1], x_sample.shape[1]))
    return _encoder(x_prompt, p, tables), _encoder(x_sample, p, tables)
```

```python
import functools

import jax
import jax.numpy as jnp
import numpy as np
from jax import lax
from jax.experimental import pallas as pl
from jax.experimental.pallas import tpu as pltpu

D_MODEL = 1024
D_FF = 2816
HEAD_DIM = 64
DILATIONS = (1, 4, 16)
BAND_RADIUS = 64
A_GROUP_HEADS = 4
A_GROUP_WIDTH = A_GROUP_HEADS * HEAD_DIM
A_WIDTH = 3 * A_GROUP_WIDTH
B_HEADS = 8
B_WIDTH = B_HEADS * HEAD_DIM
GRID_W = 64
NA_ROWS = 8
NA_COLS = 16
ROPE_THETA = 500000.0
ROT_DIM = HEAD_DIM // 4
RMS_EPS = 1e-6
NEG = -1e30

LANES = 128
VMEM_LIMIT = 56 * 1024 * 1024

TOKEN_TILE = 1024
DEINTERLEAVE_STRIDE = 4
FFN_TILE = 2048
FFN_SUB = 256
FF_CHUNKS = ((0, 1024), (1024, 1024), (2048, 768))
BAND_Q_BLOCK = 4096
BAND_Q_TILE = 128
LSE_LANES = LANES // A_GROUP_HEADS
OL_WIDTH = A_GROUP_WIDTH + LANES
B_ROWS_PER_STEP = 32
B_HALF_HEADS = 4
B_HALF_WIDTH = B_HALF_HEADS * HEAD_DIM

BF16 = jnp.bfloat16
F32 = jnp.float32


def _rms(x, g):
    return x * lax.rsqrt(jnp.mean(x * x, axis=-1, keepdims=True) + RMS_EPS) * g


def _const_spec(shape):
    zeros = (0,) * len(shape)
    return pl.BlockSpec(shape, lambda *_: zeros, pipeline_mode=pl.Buffered(1))


def _params():
    return pltpu.CompilerParams(vmem_limit_bytes=VMEM_LIMIT)


def _ffn_kernel(x_ref, pre_ref, post_ref, wg_ref, wu_ref, wd_ref, o_ref):
    n_sub = FFN_TILE // FFN_SUB

    def rows(t):
        return slice(t * FFN_SUB, (t + 1) * FFN_SUB)

    def pre(t):
        return _rms(x_ref[rows(t), :], pre_ref[...]).astype(BF16)

    def finish(t, y):
        o_ref[rows(t), :] = x_ref[rows(t), :] + 0.5 * _rms(y, post_ref[...])

    h = pre(0)
    y_prev = None
    for t in range(n_sub):
        y = None
        h_next = None
        for c, (start, size) in enumerate(FF_CHUNKS):
            g = jnp.dot(h, wg_ref[:, start:start + size], preferred_element_type=F32)
            u = jnp.dot(h, wu_ref[:, start:start + size], preferred_element_type=F32)
            a = (g * jax.nn.sigmoid(g) * u).astype(BF16)
            part = jnp.dot(a, wd_ref[start:start + size, :], preferred_element_type=F32)
            y = part if y is None else y + part
            if c == 0:
                if t + 1 < n_sub:
                    h_next = pre(t + 1)
                if t > 0:
                    finish(t - 1, y_prev)
        h, y_prev = h_next, y
    finish(n_sub - 1, y_prev)


def _ffn(x2d, pre_g, post_g, wg, wu, wd):
    m = x2d.shape[0]
    return pl.pallas_call(
        _ffn_kernel,
        grid=(m // FFN_TILE,),
        in_specs=[
            pl.BlockSpec((FFN_TILE, D_MODEL), lambda i: (i, 0)),
            _const_spec((1, D_MODEL)),
            _const_spec((1, D_MODEL)),
            _const_spec((D_MODEL, D_FF)),
            _const_spec((D_MODEL, D_FF)),
            _const_spec((D_FF, D_MODEL)),
        ],
        out_specs=pl.BlockSpec((FFN_TILE, D_MODEL), lambda i: (i, 0)),
        out_shape=jax.ShapeDtypeStruct((m, D_MODEL), F32),
        compiler_params=_params(),
    )(x2d, pre_g, post_g, wg, wu, wd)


def _inproj_kernel(x_ref, pre_ref, w_ref, rot_ref,
                   a0_ref, a1_ref, a2_ref, b_ref, slab_ref, tmp_ref):
    h = _rms(x_ref[0], pre_ref[...]).astype(BF16)
    half = ROT_DIM // 2
    packed = rot_ref[...]
    head_lane = lax.broadcasted_iota(jnp.int32, packed.shape, 1) % HEAD_DIM
    cos = jnp.where(head_lane < ROT_DIM, packed, 1.0)
    sin_hi = jnp.where(head_lane < half, 0.0,
                       jnp.where(head_lane < ROT_DIM, pltpu.roll(packed, LANES - half, 1), 0.0))
    sin_lo = jnp.where(head_lane < half, pltpu.roll(packed, LANES - (ROT_DIM + half), 1), 0.0)
    n_slabs = A_WIDTH // LANES
    n_rot_slabs = 2 * A_GROUP_WIDTH // LANES
    for g, (d, out_ref) in enumerate(zip(DILATIONS, (a0_ref, a1_ref, a2_ref))):
        res = jnp.dot(h, w_ref[:, g * A_WIDTH:(g + 1) * A_WIDTH], preferred_element_type=F32)
        n = TOKEN_TILE // d
        for c in range(n_slabs):
            xs = res[:, c * LANES:(c + 1) * LANES]
            if c < n_rot_slabs:
                xs = (xs * cos + pltpu.roll(xs, ROT_DIM // 2, 1) * sin_hi
                      + pltpu.roll(xs, LANES - ROT_DIM // 2, 1) * sin_lo)
            if d == 1:
                out_ref[0, 0, :, c * LANES:(c + 1) * LANES] = xs.astype(BF16)
            else:
                slab_ref[c] = xs
        if d == DEINTERLEAVE_STRIDE:
            for c in range(n_slabs):
                for r in range(d):
                    rows = slab_ref[c, pl.ds(r, n, stride=d), :]
                    out_ref[0, r, :, c * LANES:(c + 1) * LANES] = rows.astype(BF16)
        elif d > 1:
            st = DEINTERLEAVE_STRIDE
            quarter = TOKEN_TILE // st
            for c in range(n_slabs):
                for r0 in range(st):
                    tmp_ref[c, r0 * quarter:(r0 + 1) * quarter, :] = slab_ref[c, pl.ds(r0, quarter, stride=st), :]
                for r1 in range(d // st):
                    for r0 in range(st):
                        rows = tmp_ref[c, pl.ds(r0 * quarter + r1, n, stride=st), :]
                        out_ref[0, st * r1 + r0, :, c * LANES:(c + 1) * LANES] = rows.astype(BF16)
    res = jnp.dot(h, w_ref[:, 3 * A_WIDTH:], preferred_element_type=F32)
    b_ref[0] = res.astype(BF16)


def _inproj(x, pre_g, w_proj, rot_table):
    bsz, seq, _ = x.shape
    nblk = seq // TOKEN_TILE
    tab_spec = pl.BlockSpec((TOKEN_TILE, LANES), lambda b, i: (i, 0))
    out_shapes = [jax.ShapeDtypeStruct((bsz, d, seq // d, A_WIDTH), BF16) for d in DILATIONS]
    out_specs = [pl.BlockSpec((1, d, TOKEN_TILE // d, A_WIDTH), lambda b, i: (b, 0, i, 0)) for d in DILATIONS]
    out_shapes.append(jax.ShapeDtypeStruct((bsz, seq, 3 * B_WIDTH), BF16))
    out_specs.append(pl.BlockSpec((1, TOKEN_TILE, 3 * B_WIDTH), lambda b, i: (b, i, 0)))
    return pl.pallas_call(
        _inproj_kernel,
        grid=(bsz, nblk),
        in_specs=[
            pl.BlockSpec((1, TOKEN_TILE, D_MODEL), lambda b, i: (b, i, 0)),
            _const_spec((1, D_MODEL)),
            _const_spec((D_MODEL, 3 * A_WIDTH + 3 * B_WIDTH)),
            tab_spec,
        ],
        out_specs=out_specs,
        out_shape=out_shapes,
        scratch_shapes=[pltpu.VMEM((A_WIDTH // LANES, TOKEN_TILE, LANES), F32)] * 2,
        compiler_params=_params(),
    )(x, pre_g, w_proj, rot_table)


def _head_stack(q, n_heads):
    rows, width = q.shape
    lane_head = lax.broadcasted_iota(jnp.int32, (rows, width), 1) // HEAD_DIM
    zero = jnp.zeros_like(q)
    return jnp.concatenate([jnp.where(lane_head == hh, q, zero) for hh in range(n_heads)], axis=0)


def _softmax_parts(s):
    m = jnp.max(s, axis=-1, keepdims=True)
    p = jnp.exp(s - m)
    l = jnp.sum(p, axis=-1, keepdims=True)
    return p.astype(BF16), 1.0 / l, m + jnp.log(l)


def _band_bias():
    r = BAND_RADIUS
    q_off = np.arange(BAND_Q_TILE)[:, None]
    col = np.arange(BAND_Q_TILE + 2 * r)[None, :]
    band = (col >= q_off) & (col <= q_off + 2 * r)
    variants = []
    for v in range(4):
        ok = band
        if v & 1:
            ok = ok & (col >= r)
        if v & 2:
            ok = ok & (col < BAND_Q_TILE + r)
        variants.append(np.where(ok, 0.0, NEG).astype(np.float32))
    return jnp.asarray(np.stack(variants))


def _attn_a_kernel(q_ref, kp_ref, kc_ref, kn_ref, vp_ref, vc_ref, vn_ref, bias_ref, ol_ref, k_ext, v_ext,
                   *, q_block, n_seqs):
    i = pl.program_id(1)
    last = pl.num_programs(1) - 1
    r = BAND_RADIUS
    for g in range(n_seqs):
        k_ext[g, 0:r] = kp_ref[g]
        k_ext[g, r:r + q_block] = kc_ref[g]
        k_ext[g, r + q_block:] = kn_ref[g]
        v_ext[g, 0:r] = vp_ref[g]
        v_ext[g, r:r + q_block] = vc_ref[g]
        v_ext[g, r + q_block:] = vn_ref[g]
    n_keys = BAND_Q_TILE + 2 * r
    n_tiles = q_block // BAND_Q_TILE
    units = [(g, j) for g in range(n_seqs) for j in range(n_tiles)]

    def scores(g, j):
        q = _head_stack(q_ref[g, j * BAND_Q_TILE:(j + 1) * BAND_Q_TILE, :], A_GROUP_HEADS)
        k = k_ext[g, j * BAND_Q_TILE:j * BAND_Q_TILE + n_keys, :]
        return lax.dot_general(q, k, (((1,), (1,)), ((), ())), preferred_element_type=F32)

    s_next = scores(*units[0])
    for u, (g, j) in enumerate(units):
        variant = 0
        if j == 0:
            variant = variant + jnp.where(i == 0, 1, 0)
        if j == n_tiles - 1:
            variant = variant + jnp.where(i == last, 2, 0)
        v = v_ext[g, j * BAND_Q_TILE:j * BAND_Q_TILE + n_keys, :]
        s = s_next
        if u + 1 < len(units):
            s_next = scores(*units[u + 1])
        sl = slice(j * BAND_Q_TILE, (j + 1) * BAND_Q_TILE)
        probs, scales = [], []
        for hh in range(A_GROUP_HEADS):
            p, inv_l, lse_h = _softmax_parts(s[hh * BAND_Q_TILE:(hh + 1) * BAND_Q_TILE] + bias_ref[variant])
            probs.append(p)
            scales.append(inv_l)
            lse_lanes = slice(A_GROUP_WIDTH + hh * LSE_LANES, A_GROUP_WIDTH + (hh + 1) * LSE_LANES)
            ol_ref[g, sl, lse_lanes] = jnp.broadcast_to(lse_h, (BAND_Q_TILE, LSE_LANES))
        o = jnp.dot(jnp.concatenate(probs, axis=0), v, preferred_element_type=F32)
        for hh in range(A_GROUP_HEADS):
            lanes = slice(hh * HEAD_DIM, (hh + 1) * HEAD_DIM)
            ol_ref[g, sl, lanes] = o[hh * BAND_Q_TILE:(hh + 1) * BAND_Q_TILE, lanes] * scales[hh]


def _attn_a(qkv, band_bias):
    n_seq, seq_len, _ = qkv.shape
    q_block = min(BAND_Q_BLOCK, seq_len)
    n_seqs = min(n_seq, BAND_Q_BLOCK // q_block)
    assert n_seq % n_seqs == 0 and seq_len % q_block == 0
    halo_per_block = q_block // BAND_RADIUS
    n_halo = seq_len // BAND_RADIUS

    def cur(col):
        return pl.BlockSpec((n_seqs, q_block, A_GROUP_WIDTH), lambda s, i: (s, i, col))

    def prev(col):
        return pl.BlockSpec((n_seqs, BAND_RADIUS, A_GROUP_WIDTH),
                            lambda s, i: (s, jnp.maximum(i * halo_per_block - 1, 0), col))

    def nxt(col):
        return pl.BlockSpec((n_seqs, BAND_RADIUS, A_GROUP_WIDTH),
                            lambda s, i: (s, jnp.minimum((i + 1) * halo_per_block, n_halo - 1), col))

    ext = pltpu.VMEM((n_seqs, q_block + 2 * BAND_RADIUS, A_GROUP_WIDTH), BF16)
    return pl.pallas_call(
        functools.partial(_attn_a_kernel, q_block=q_block, n_seqs=n_seqs),
        grid=(n_seq // n_seqs, seq_len // q_block),
        in_specs=[cur(0), prev(1), cur(1), nxt(1), prev(2), cur(2), nxt(2), _const_spec(band_bias.shape)],
        out_specs=pl.BlockSpec((n_seqs, q_block, OL_WIDTH), lambda s, i: (s, i, 0)),
        out_shape=jax.ShapeDtypeStruct((n_seq, seq_len, OL_WIDTH), F32),
        scratch_shapes=[ext, ext],
        compiler_params=_params(),
    )(qkv, qkv, qkv, qkv, qkv, qkv, qkv, band_bias)


def _attn_b_kernel(q_ref, kp_ref, kc_ref, kn_ref, vp_ref, vc_ref, vn_ref, bias_ref, o_ref, k_ext, v_ext):
    i = pl.program_id(1)
    last = pl.num_programs(1) - 1
    halo = (NA_ROWS // 2) * GRID_W
    step_tokens = B_ROWS_PER_STEP * GRID_W
    win = NA_ROWS * GRID_W
    k_ext[0:halo] = kp_ref[0]
    k_ext[halo:halo + step_tokens] = kc_ref[0]
    k_ext[halo + step_tokens:] = kn_ref[0]
    v_ext[0:halo] = vp_ref[0]
    v_ext[halo:halo + step_tokens] = vc_ref[0]
    v_ext[halo + step_tokens:] = vn_ref[0]
    lo = jnp.where(i == 0, NA_ROWS // 2, 0)
    hi = jnp.where(i == last, B_ROWS_PER_STEP - NA_ROWS // 2, B_ROWS_PER_STEP - 1)
    n_halves = B_HEADS // B_HALF_HEADS
    units = [(rr, half) for rr in range(B_ROWS_PER_STEP) for half in range(n_halves)]

    def window(rr):
        w_row = jnp.clip(rr, lo, hi)
        start = pl.multiple_of(w_row * GRID_W, GRID_W)
        delta = w_row - rr + (NA_ROWS // 2 - 1)
        return start, delta

    def scores(rr, half):
        start, _ = window(rr)
        lanes = slice(half * B_HALF_WIDTH, (half + 1) * B_HALF_WIDTH)
        q = _head_stack(q_ref[0, rr * GRID_W:(rr + 1) * GRID_W, lanes], B_HALF_HEADS)
        k = k_ext[pl.ds(start, win), lanes]
        return lax.dot_general(q, k, (((1,), (1,)), ((), ())), preferred_element_type=F32)

    s_next = scores(*units[0])
    for u, (rr, half) in enumerate(units):
        start, delta = window(rr)
        rows = slice(rr * GRID_W, (rr + 1) * GRID_W)
        lanes = slice(half * B_HALF_WIDTH, (half + 1) * B_HALF_WIDTH)
        s = s_next
        if u + 1 < len(units):
            s_next = scores(*units[u + 1])
        probs, scales = [], []
        for hh in range(B_HALF_HEADS):
            b_rows = pl.ds((half * B_HALF_HEADS + hh) * GRID_W, GRID_W)
            p, inv_l, _ = _softmax_parts(s[hh * GRID_W:(hh + 1) * GRID_W] + bias_ref[delta, b_rows, :])
            probs.append(p)
            scales.append(inv_l)
        v = v_ext[pl.ds(start, win), lanes]
        o = jnp.dot(jnp.concatenate(probs, axis=0), v, preferred_element_type=F32)
        for hh in range(B_HALF_HEADS):
            h_lanes = slice(hh * HEAD_DIM, (hh + 1) * HEAD_DIM)
            o_h = o[hh * GRID_W:(hh + 1) * GRID_W, h_lanes] * scales[hh]
            col = half * B_HALF_WIDTH + hh * HEAD_DIM
            o_ref[0, rows, col:col + HEAD_DIM] = o_h.astype(BF16)


def _attn_b(qkv, bias):
    bsz, seq, _ = qkv.shape
    step_tokens = B_ROWS_PER_STEP * GRID_W
    halo = (NA_ROWS // 2) * GRID_W
    per = step_tokens // halo
    n_halo = seq // halo

    def cur(col):
        return pl.BlockSpec((1, step_tokens, B_WIDTH), lambda b, i: (b, i, col))

    def prev(col):
        return pl.BlockSpec((1, halo, B_WIDTH), lambda b, i: (b, jnp.maximum(i * per - 1, 0), col))

    def nxt(col):
        return pl.BlockSpec((1, halo, B_WIDTH), lambda b, i: (b, jnp.minimum((i + 1) * per, n_halo - 1), col))

    ext = pltpu.VMEM((step_tokens + 2 * halo, B_WIDTH), BF16)
    return pl.pallas_call(
        _attn_b_kernel,
        grid=(bsz, seq // step_tokens),
        in_specs=[cur(0), prev(1), cur(1), nxt(1), prev(2), cur(2), nxt(2),
                  _const_spec(bias.shape)],
        out_specs=pl.BlockSpec((1, step_tokens, B_WIDTH), lambda b, i: (b, i, 0)),
        out_shape=jax.ShapeDtypeStruct((bsz, seq, B_WIDTH), BF16),
        scratch_shapes=[ext, ext],
        compiler_params=_params(),
    )(qkv, qkv, qkv, qkv, qkv, qkv, qkv, bias)


MIX_SUB = 256


def _mixout_kernel(x_ref, ol0_ref, ol1_ref, ol2_ref, ob_ref,
                   pre_ref, post_ref, wgate_ref, bgate_ref, wa_ref, wb_ref, wout_ref,
                   y_ref, nat_ref, tmp_ref):
    n_slabs = OL_WIDTH // LANES
    o_slabs = A_GROUP_WIDTH // LANES
    for k, (d, src) in enumerate(((DILATIONS[1], ol1_ref), (DILATIONS[2], ol2_ref))):
        n = TOKEN_TILE // d
        st = DEINTERLEAVE_STRIDE
        quarter = TOKEN_TILE // st
        for c in range(n_slabs):
            dst = k * n_slabs + c
            if d == st:
                for r in range(d):
                    nat_ref[dst, pl.ds(r, n, stride=d), :] = src[0, r, :, c * LANES:(c + 1) * LANES]
            else:
                for r1 in range(d // st):
                    for r0 in range(st):
                        tmp_ref[c, pl.ds(r0 * quarter + r1, n, stride=st), :] = \
                            src[0, st * r1 + r0, :, c * LANES:(c + 1) * LANES]
                for r0 in range(st):
                    nat_ref[dst, pl.ds(r0, quarter, stride=st), :] = tmp_ref[c, r0 * quarter:(r0 + 1) * quarter, :]

    def nat_o(k, sl):
        return jnp.concatenate([nat_ref[k * n_slabs + c, sl, :] for c in range(o_slabs)], axis=-1)

    def nat_lse(k, sl):
        return nat_ref[k * n_slabs + o_slabs, sl, :]

    def spread_heads(c):
        lane = lax.broadcasted_iota(jnp.int32, c.shape, 1)
        by1, by2, by3 = (pltpu.roll(c, s * LSE_LANES, 1) for s in (1, 2, 3))
        first, last = lane < LSE_LANES, lane >= 3 * LSE_LANES
        lo = jnp.where(first, c, jnp.where(last, by2, by1))
        hi = jnp.where(first, by2, jnp.where(last, c, by3))
        return jnp.concatenate([lo, hi], axis=-1)

    for t in range(TOKEN_TILE // MIX_SUB):
        sl = slice(t * MIX_SUB, (t + 1) * MIX_SUB)
        x = x_ref[0, sl, :]
        h = _rms(x, pre_ref[...]).astype(BF16)
        gates = jax.nn.sigmoid(jnp.dot(h, wgate_ref[...], preferred_element_type=F32) + bgate_ref[...])
        o0, l0 = ol0_ref[0, 0, sl, :A_GROUP_WIDTH], ol0_ref[0, 0, sl, A_GROUP_WIDTH:]
        o1, l1, o2, l2 = nat_o(0, sl), nat_lse(0, sl), nat_o(1, sl), nat_lse(1, sl)
        mx = jnp.maximum(jnp.maximum(l0, l1), l2)
        e0, e1, e2 = jnp.exp(l0 - mx), jnp.exp(l1 - mx), jnp.exp(l2 - mx)
        inv = 1.0 / (e0 + e1 + e2)
        oa = (spread_heads(e0 * inv) * o0 + spread_heads(e1 * inv) * o1 + spread_heads(e2 * inv) * o2)
        ya = jnp.dot(oa.astype(BF16), wa_ref[...], preferred_element_type=F32)
        yb = jnp.dot(ob_ref[0, sl, :], wb_ref[...], preferred_element_type=F32)
        mixed = (gates[:, :D_MODEL] * ya + gates[:, D_MODEL:] * yb).astype(BF16)
        m = jnp.dot(mixed, wout_ref[...], preferred_element_type=F32)
        y_ref[0, sl, :] = x + _rms(m, post_ref[...])


def _mixout(x, ols, ob, pre_g, post_g, w_gate, b_gate, w_a, w_b, w_out):
    bsz, seq, _ = x.shape
    nblk = seq // TOKEN_TILE
    tile_spec = pl.BlockSpec((1, TOKEN_TILE, D_MODEL), lambda b, i: (b, i, 0))
    in_specs = [tile_spec]
    operands = [x]
    for d, ol in zip(DILATIONS, ols):
        in_specs.append(pl.BlockSpec((1, d, TOKEN_TILE // d, OL_WIDTH), lambda b, i: (b, 0, i, 0)))
        operands.append(ol)
    in_specs.append(pl.BlockSpec((1, TOKEN_TILE, B_WIDTH), lambda b, i: (b, i, 0)))
    operands.append(ob)
    for w in (pre_g, post_g, w_gate, b_gate, w_a, w_b, w_out):
        in_specs.append(_const_spec(w.shape))
        operands.append(w)
    return pl.pallas_call(
        _mixout_kernel,
        grid=(bsz, nblk),
        in_specs=in_specs,
        out_specs=tile_spec,
        out_shape=jax.ShapeDtypeStruct((bsz, seq, D_MODEL), F32),
        scratch_shapes=[pltpu.VMEM((2 * OL_WIDTH // LANES, TOKEN_TILE, LANES), F32),
                        pltpu.VMEM((OL_WIDTH // LANES, TOKEN_TILE, LANES), F32)],
        compiler_params=_params(),
    )(*operands)


def _rotary_tables(seq):
    inv_freq = ROPE_THETA ** (-jnp.arange(0, ROT_DIM, 2, dtype=F32) / ROT_DIM)
    ang = jnp.arange(seq, dtype=F32)[:, None] * inv_freq[None, :]
    cos, sin = jnp.cos(ang), jnp.sin(ang)
    unused = jnp.ones((seq, HEAD_DIM - 2 * ROT_DIM), F32)
    per_head = jnp.concatenate([cos, cos, sin, -sin, unused], axis=1)
    return jnp.tile(per_head, (1, LANES // HEAD_DIM))


def _neighbourhood_bias(rpb):
    cj = np.arange(GRID_W)
    cs = np.clip(cj - NA_COLS // 2, 0, GRID_W - NA_COLS)
    col_valid = (cj[None, :] >= cs[:, None]) & (cj[None, :] < cs[:, None] + NA_COLS)
    dc_idx = np.clip(cj[None, :] - cj[:, None] + (NA_COLS - 1), 0, 2 * NA_COLS - 2)
    n_dc = 2 * NA_COLS - 1
    onehot = (dc_idx[None] == np.arange(n_dc)[:, None, None]).astype(np.float32)
    base = jnp.einsum("hrd,dqk->hrqk", rpb.astype(F32), jnp.asarray(onehot), precision=lax.Precision.HIGHEST)
    base = jnp.where(col_valid[None, None], base, NEG)
    tables = []
    for first in range(NA_ROWS):
        t = base[:, first:first + NA_ROWS].transpose(0, 2, 1, 3)
        tables.append(t.reshape(B_HEADS * GRID_W, NA_ROWS * GRID_W))
    return jnp.stack(tables, axis=0)


def _split_w_in(w_in):
    scale = HEAD_DIM ** -0.5
    o = 0
    qa = w_in[:, o:o + A_WIDTH] * scale; o += A_WIDTH
    ka = w_in[:, o:o + A_WIDTH]; o += A_WIDTH
    va = w_in[:, o:o + A_WIDTH]; o += A_WIDTH
    qb = w_in[:, o:o + B_WIDTH] * scale; o += B_WIDTH
    kb = w_in[:, o:o + B_WIDTH]; o += B_WIDTH
    vb = w_in[:, o:o + B_WIDTH]; o += B_WIDTH
    w_gate = w_in[:, o:]
    cols = []
    for g in range(len(DILATIONS)):
        sl = slice(g * A_GROUP_WIDTH, (g + 1) * A_GROUP_WIDTH)
        cols += [qa[:, sl], ka[:, sl], va[:, sl]]
    cols += [qb, kb, vb]
    return jnp.concatenate(cols, axis=1).astype(BF16), w_gate.astype(BF16)


def _encoder(x, p, tables):
    bsz, seq, _ = x.shape
    row = lambda v: v.reshape(1, -1)
    x1 = _ffn(x.reshape(bsz * seq, D_MODEL), row(p["ffn1_pre_g"]), row(p["ffn1_post_g"]),
              p["ffn1_wg"], p["ffn1_wu"], p["ffn1_wd"]).reshape(bsz, seq, D_MODEL)
    a0, a1, a2, qkvb = _inproj(x1, row(p["mix_pre_g"]), p["w_proj"], tables)
    ols = []
    band_bias = _band_bias()
    for d, qkv in zip(DILATIONS, (a0, a1, a2)):
        ol = _attn_a(qkv.reshape(bsz * d, seq // d, A_WIDTH), band_bias)
        ols.append(ol.reshape(bsz, d, seq // d, OL_WIDTH))
    ob = _attn_b(qkvb, p["bias_b"])
    x2 = _mixout(x1, ols, ob, row(p["mix_pre_g"]), row(p["mix_post_g"]), p["w_gate"], row(p["b_gate"]),
                 p["w_a"], p["w_b"], p["w_out"])
    x3 = _ffn(x2.reshape(bsz * seq, D_MODEL), row(p["ffn2_pre_g"]), row(p["ffn2_post_g"]),
              p["ffn2_wg"], p["ffn2_wu"], p["ffn2_wd"])
    return x3.reshape(bsz, seq, D_MODEL)


def kernel(x_prompt, x_sample, ffn1_pre_g, ffn1_post_g, ffn1_w_gate, ffn1_w_up, ffn1_w_down, mix_pre_g, mix_post_g,
           w_in, b_gate, rpb, w_branch_a, w_branch_b, w_out, ffn2_pre_g, ffn2_post_g, ffn2_w_gate, ffn2_w_up,
           ffn2_w_down):
    w_proj, w_gate = _split_w_in(w_in[0])
    p = {
        "ffn1_pre_g": ffn1_pre_g[0], "ffn1_post_g": ffn1_post_g[0],
        "ffn1_wg": ffn1_w_gate[0].astype(BF16), "ffn1_wu": ffn1_w_up[0].astype(BF16),
        "ffn1_wd": ffn1_w_down[0].astype(BF16),
        "mix_pre_g": mix_pre_g[0], "mix_post_g": mix_post_g[0],
        "w_proj": w_proj, "w_gate": w_gate, "b_gate": b_gate[0],
        "bias_b": _neighbourhood_bias(rpb[0]),
        "w_a": w_branch_a[0].astype(BF16), "w_b": w_branch_b[0].astype(BF16), "w_out": w_out[0].astype(BF16),
        "ffn2_pre_g": ffn2_pre_g[0], "ffn2_post_g": ffn2_post_g[0],
        "ffn2_wg": ffn2_w_gate[0].astype(BF16), "ffn2_wu": ffn2_w_up[0].astype(BF16),
        "ffn2_wd": ffn2_w_down[0].astype(BF16),
    }
    tables = _rotary_tables(max(x_prompt.shape[1], x_sample.shape[1]))
    return _encoder(x_prompt, p, tables), _encoder(x_sample, p, tables)
```
